```python
import jax, jax.numpy as jnp
from jax import lax
import numpy as np

D_MODEL = 1024
BATCH = 2
SEQ = 16384
DEPTH = 2
DEC_BATCH = 8
DEC_SEQ = 64
PAST_LEN = 2048

CHUNK = 64
LEFT_CHUNKS = 8
LEFT_CTX = LEFT_CHUNKS * CHUNK
BAND = LEFT_CTX + CHUNK
N_HEADS = 16
HEAD_DIM = D_MODEL // N_HEADS
REL_CLIP = 128
N_REL = 2 * REL_CLIP + 1
CONV_WIDTH = 31
D_FF = ((8 * D_MODEL // 3 + 127) // 128) * 128
N_ATTN_LAYERS = (DEPTH + 1) // 2
N_CONV_LAYERS = DEPTH // 2
EPS = 1e-6
NEG_INF = -1e30

kernel_name = 'chunk_stream_conformer_hybrid'


def rms_norm(x, g):
    xf = x.astype(jnp.float32)
    y = xf * lax.rsqrt(jnp.mean(xf * xf, axis=-1, keepdims=True) + EPS)
    return (y * g.astype(jnp.float32)).astype(x.dtype)


def layer_norm(x, g, b):
    xf = x.astype(jnp.float32)
    mu = jnp.mean(xf, axis=-1, keepdims=True)
    xc = xf - mu
    y = xc * lax.rsqrt(jnp.mean(xc * xc, axis=-1, keepdims=True) + EPS)
    return (y * g.astype(jnp.float32) + b.astype(jnp.float32)).astype(x.dtype)


def swiglu_ffn(x, g, w_gate, w_up, w_down):
    h = rms_norm(x, g)
    return (jax.nn.silu(h @ w_gate) * (h @ w_up)) @ w_down


def rel_bias(table):
    i = jnp.arange(CHUNK)[:, None]
    j = jnp.arange(BAND)[None, :]
    dist = LEFT_CTX + i - j
    idx = jnp.clip(dist, -REL_CLIP, REL_CLIP) + REL_CLIP
    return jnp.transpose(table[idx], (2, 0, 1)).astype(jnp.float32)


def chunk_band_attention(x, cache_k, cache_v, norm_g, w_qkv, q_gain, k_gain, rel_table, w_o):
    bsz, t, _ = x.shape
    past = cache_k.shape[1]
    n_chunks = -(-t // CHUNK)
    t_pad = n_chunks * CHUNK
    h = rms_norm(x, norm_g)
    q, k, v = jnp.split(h @ w_qkv, 3, axis=-1)
    q = rms_norm(q.reshape(bsz, t, N_HEADS, HEAD_DIM), q_gain)
    k = rms_norm(k.reshape(bsz, t, N_HEADS, HEAD_DIM), k_gain)
    v = v.reshape(bsz, t, N_HEADS, HEAD_DIM)
    qp = jnp.pad(q, ((0, 0), (0, t_pad - t), (0, 0), (0, 0)))
    pad_kv = ((0, 0), (LEFT_CTX, t_pad - t), (0, 0), (0, 0))
    kp = jnp.pad(jnp.concatenate([cache_k.astype(k.dtype), k], axis=1), pad_kv)
    vp = jnp.pad(jnp.concatenate([cache_v.astype(v.dtype), v], axis=1), pad_kv)
    bias = rel_bias(rel_table)
    scale = HEAD_DIM ** -0.5
    n_keys = past + t

    def one_chunk(c):
        start = c * CHUNK
        q_c = lax.dynamic_slice_in_dim(qp, start, CHUNK, axis=1)
        k_b = lax.dynamic_slice_in_dim(kp, past + start, BAND, axis=1)
        v_b = lax.dynamic_slice_in_dim(vp, past + start, BAND, axis=1)
        kpos = past + start - LEFT_CTX + jnp.arange(BAND)
        valid = (kpos >= 0) & (kpos < n_keys)
        s = jnp.einsum('bqhd,bkhd->bhqk', q_c, k_b).astype(jnp.float32) * scale + bias
        s = jnp.where(valid, s, NEG_INF)
        p = jax.nn.softmax(s, axis=-1).astype(v_b.dtype)
        return jnp.einsum('bhqk,bkhd->bqhd', p, v_b)

    o = lax.map(one_chunk, jnp.arange(n_chunks))
    o = jnp.moveaxis(o, 0, 1).reshape(bsz, t_pad, D_MODEL)[:, :t]
    return o @ w_o, k, v


def conformer_conv(x, buf, norm_g, w_pw1, b_pw1, w_dw, b_dw, ln_g, ln_b, w_pw2, b_pw2):
    h = rms_norm(x, norm_g)
    a, gate = jnp.split(h @ w_pw1 + b_pw1, 2, axis=-1)
    u = a * jax.nn.sigmoid(gate)
    ext = jnp.concatenate([buf.astype(u.dtype), u], axis=1)
    y = lax.conv_general_dilated(ext, w_dw[:, None, :].astype(ext.dtype), (1,), 'VALID',
                                 dimension_numbers=('NWC', 'WIO', 'NWC'),
                                 feature_group_count=D_MODEL) + b_dw
    y = jax.nn.silu(layer_norm(y, ln_g, ln_b))
    return y @ w_pw2 + b_pw2, ext[:, -(CONV_WIDTH - 1):]


def trunk(x, k_caches, v_caches, conv_bufs, p):
    new_k, new_v, new_conv = [], [], []
    for i in range(DEPTH):
        x = x + 0.5 * swiglu_ffn(x, p['ffn1_norm'][i], p['ffn1_w_gate'][i], p['ffn1_w_up'][i], p['ffn1_w_down'][i])
        if i % 2 == 0:
            a = i // 2
            out, k, v = chunk_band_attention(x, k_caches[a], v_caches[a], p['attn_norm'][a], p['attn_w_qkv'][a],
                                             p['attn_q_gain'][a], p['attn_k_gain'][a], p['attn_rel_bias'][a],
                                             p['attn_w_o'][a])
            new_k.append(k)
            new_v.append(v)
        else:
            b = i // 2
            out, nb = conformer_conv(x, conv_bufs[b], p['conv_norm'][b], p['conv_w_pw1'][b], p['conv_b_pw1'][b],
                                     p['conv_w_dw'][b], p['conv_b_dw'][b], p['conv_ln_g'][b], p['conv_ln_b'][b],
                                     p['conv_w_pw2'][b], p['conv_b_pw2'][b])
            new_conv.append(nb)
        x = x + out
        x = x + 0.5 * swiglu_ffn(x, p['ffn2_norm'][i], p['ffn2_w_gate'][i], p['ffn2_w_up'][i], p['ffn2_w_down'][i])
    return x, jnp.stack(new_k), jnp.stack(new_v), jnp.stack(new_conv)


def setup_inputs(seed: int = 0) -> dict:
    key = jax.random.key(seed)
    ks = iter(jax.random.split(key, 40))
    f32 = jnp.float32

    def nrm(shape, scale):
        return jax.random.normal(next(ks), shape, f32) * scale

    def gain(shape):
        return 1.0 + nrm(shape, 0.05)

    cache_len = min(LEFT_CTX, PAST_LEN)
    A, C = N_ATTN_LAYERS, N_CONV_LAYERS
    return {
        'x_prompt': nrm((BATCH, SEQ, D_MODEL), 1.0),
        'x_sample': nrm((DEC_BATCH, DEC_SEQ, D_MODEL), 1.0),
        'cache_attn_k': nrm((A, DEC_BATCH, cache_len, N_HEADS, HEAD_DIM), 1.0),
        'cache_attn_v': nrm((A, DEC_BATCH, cache_len, N_HEADS, HEAD_DIM), 1.0),
        'state_conv': nrm((C, DEC_BATCH, CONV_WIDTH - 1, D_MODEL), 0.5),
        'ffn1_norm': gain((DEPTH, D_MODEL)),
        'ffn1_w_gate': nrm((DEPTH, D_MODEL, D_FF), D_MODEL ** -0.5),
        'ffn1_w_up': nrm((DEPTH, D_MODEL, D_FF), D_MODEL ** -0.5),
        'ffn1_w_down': nrm((DEPTH, D_FF, D_MODEL), D_FF ** -0.5),
        'ffn2_norm': gain((DEPTH, D_MODEL)),
        'ffn2_w_gate': nrm((DEPTH, D_MODEL, D_FF), D_MODEL ** -0.5),
        'ffn2_w_up': nrm((DEPTH, D_MODEL, D_FF), D_MODEL ** -0.5),
        'ffn2_w_down': nrm((DEPTH, D_FF, D_MODEL), D_FF ** -0.5),
        'attn_norm': gain((A, D_MODEL)),
        'attn_w_qkv': nrm((A, D_MODEL, 3 * D_MODEL), D_MODEL ** -0.5),
        'attn_q_gain': gain((A, HEAD_DIM)),
        'attn_k_gain': gain((A, HEAD_DIM)),
        'attn_rel_bias': nrm((A, N_REL, N_HEADS), 0.5),
        'attn_w_o': nrm((A, D_MODEL, D_MODEL), D_MODEL ** -0.5),
        'conv_norm': gain((C, D_MODEL)),
        'conv_w_pw1': nrm((C, D_MODEL, 2 * D_MODEL), D_MODEL ** -0.5),
        'conv_b_pw1': nrm((C, 2 * D_MODEL), 0.02),
        'conv_w_dw': nrm((C, CONV_WIDTH, D_MODEL), CONV_WIDTH ** -0.5),
        'conv_b_dw': nrm((C, D_MODEL), 0.02),
        'conv_ln_g': gain((C, D_MODEL)),
        'conv_ln_b': nrm((C, D_MODEL), 0.02),
        'conv_w_pw2': nrm((C, D_MODEL, D_MODEL), D_MODEL ** -0.5),
        'conv_b_pw2': nrm((C, D_MODEL), 0.02),
    }


def reference(x_prompt, x_sample, cache_attn_k, cache_attn_v, state_conv,
              ffn1_norm, ffn1_w_gate, ffn1_w_up, ffn1_w_down,
              ffn2_norm, ffn2_w_gate, ffn2_w_up, ffn2_w_down,
              attn_norm, attn_w_qkv, attn_q_gain, attn_k_gain, attn_rel_bias, attn_w_o,
              conv_norm, conv_w_pw1, conv_b_pw1, conv_w_dw, conv_b_dw, conv_ln_g, conv_ln_b,
              conv_w_pw2, conv_b_pw2):
    p = dict(ffn1_norm=ffn1_norm, ffn1_w_gate=ffn1_w_gate, ffn1_w_up=ffn1_w_up, ffn1_w_down=ffn1_w_down,
             ffn2_norm=ffn2_norm, ffn2_w_gate=ffn2_w_gate, ffn2_w_up=ffn2_w_up, ffn2_w_down=ffn2_w_down,
             attn_norm=attn_norm, attn_w_qkv=attn_w_qkv, attn_q_gain=attn_q_gain, attn_k_gain=attn_k_gain,
             attn_rel_bias=attn_rel_bias, attn_w_o=attn_w_o,
             conv_norm=conv_norm, conv_w_pw1=conv_w_pw1, conv_b_pw1=conv_b_pw1, conv_w_dw=conv_w_dw,
             conv_b_dw=conv_b_dw, conv_ln_g=conv_ln_g, conv_ln_b=conv_ln_b, conv_w_pw2=conv_w_pw2,
             conv_b_pw2=conv_b_pw2)
    bp, sp = x_prompt.shape[0], x_prompt.shape[1]
    empty_kv = jnp.zeros((N_ATTN_LAYERS, bp, 0, N_HEADS, HEAD_DIM), x_prompt.dtype)
    zero_conv = jnp.zeros((N_CONV_LAYERS, bp, CONV_WIDTH - 1, D_MODEL), x_prompt.dtype)
    y_prompt, k_p, v_p, new_conv_prompt = trunk(x_prompt, empty_kv, empty_kv, zero_conv, p)
    keep = min(LEFT_CTX, sp)
    new_attn_k_prompt = k_p[:, :, sp - keep:]
    new_attn_v_prompt = v_p[:, :, sp - keep:]
    y_sample, new_attn_k_sample, new_attn_v_sample, new_conv_sample = trunk(
        x_sample, cache_attn_k, cache_attn_v, state_conv, p)
    return (y_prompt, y_sample, new_attn_k_prompt, new_attn_v_prompt,
            new_attn_k_sample, new_attn_v_sample, new_conv_prompt, new_conv_sample)
```

```python
import functools

import jax
import jax.numpy as jnp
from jax import lax
from jax.experimental import pallas as pl
from jax.experimental.pallas import tpu as pltpu

D_MODEL = 1024
N_HEADS = 16
HEAD_DIM = D_MODEL // N_HEADS
CHUNK = 64
LEFT_CHUNKS = 8
LEFT_CTX = LEFT_CHUNKS * CHUNK
REL_CLIP = 128
CONV_WIDTH = 31
CONV_HIST = CONV_WIDTH - 1
D_FF = 2816
EPS = 1e-6
NEG_INF = -1e30

BF16 = jnp.bfloat16
F32 = jnp.float32

LANES = 128
MXU_DIM = 256
VMEM_LIMIT_BYTES = 56 * 1024 * 1024

GROUP = 2 * CHUNK
N_SLOTS = LEFT_CTX // GROUP + 1
N_PAIRS = N_HEADS // 2
ATTN_SCALE = HEAD_DIM ** -0.5

FFN_TILE = 512
FF_CHUNKS = ((0, 512), (512, 1024), (1024, 1536), (1536, 2048), (2048, 2560), (2560, 2816))
CONV_ROWS = 32
CONV_PAD = 32


def _dot(a, b):
    return jnp.dot(a, b, preferred_element_type=F32)


def _dot_nt(a, b):
    return lax.dot_general(a, b, (((1,), (1,)), ((), ())), preferred_element_type=F32)


def _rms_norm(x, g):
    ms = jnp.mean(x * x, axis=-1, keepdims=True)
    return x * lax.rsqrt(ms + EPS) * g


def _const_spec(shape):
    zeros = (0,) * len(shape)
    return pl.BlockSpec(shape, lambda *_: zeros, pipeline_mode=pl.Buffered(1))


def _ffn_body(x_ref, g_ref, wg_ref, wu_ref, wd_ref, o_ref):
    x = x_ref[...]
    h = _rms_norm(x, g_ref[...]).astype(BF16)
    acc = None
    for c0, c1 in FF_CHUNKS:
        gate = _dot(h, wg_ref[:, c0:c1])
        up = _dot(h, wu_ref[:, c0:c1])
        a = (gate * jax.nn.sigmoid(gate) * up).astype(BF16)
        part = _dot(a, wd_ref[c0:c1, :])
        acc = part if acc is None else acc + part
    o_ref[...] = x + 0.5 * acc


def _ffn(x, g, wg, wu, wd):
    b, t, d = x.shape
    rows = b * t
    tm = min(FFN_TILE, rows)
    assert rows % tm == 0
    out = pl.pallas_call(
        _ffn_body,
        grid=(rows // tm,),
        in_specs=[
            pl.BlockSpec((tm, d), lambda i: (i, 0)),
            _const_spec((1, d)),
            _const_spec((d, D_FF)),
            _const_spec((d, D_FF)),
            _const_spec((D_FF, d)),
        ],
        out_specs=pl.BlockSpec((tm, d), lambda i: (i, 0)),
        out_shape=jax.ShapeDtypeStruct((rows, d), F32),
        compiler_params=pltpu.CompilerParams(
            dimension_semantics=("arbitrary",), vmem_limit_bytes=VMEM_LIMIT_BYTES),
        name="ffn",
    )(x.reshape(rows, d), g.reshape(1, d), wg, wu, wd)
    return out.reshape(b, t, d)


def _head_mean_sq(y, bd_ref):
    sq = y * y
    hi = sq.astype(BF16)
    lo = (sq - hi.astype(F32)).astype(BF16)
    bd = bd_ref[...]
    cols = []
    for c in range(0, D_MODEL, MXU_DIM):
        cols.append(_dot(hi[:, c:c + MXU_DIM], bd) + _dot(lo[:, c:c + MXU_DIM], bd))
    return jnp.concatenate(cols, axis=1)


def _attn_body(*refs, fresh):
    if fresh:
        (x_ref, ng_ref, wqkv_ref, qg_ref, kg_ref, bias_ref, wo_ref, bd_ref,
         xo_ref, ko_ref, vo_ref, hk_ref, hv_ref) = refs
        ck_ref = cv_ref = None
    else:
        (x_ref, ck_ref, cv_ref, ng_ref, wqkv_ref, qg_ref, kg_ref, bias_ref, wo_ref, bd_ref,
         xo_ref, ko_ref, vo_ref, hk_ref, hv_ref) = refs
    t = pl.program_id(1)

    lane = lax.broadcasted_iota(jnp.int32, (GROUP, D_MODEL), 1)
    even_head = (lane % LANES) < HEAD_DIM

    def put(slot, k, v):
        for ref, val in ((hk_ref, k), (hv_ref, v)):
            ref[slot, 0:GROUP, :] = jnp.where(even_head, val, 0.0).astype(BF16)
            ref[slot, GROUP:2 * GROUP, :] = jnp.where(even_head, 0.0, val).astype(BF16)

    @pl.when(t == 0)
    def _init():
        if fresh:
            hk_ref[...] = jnp.zeros(hk_ref.shape, BF16)
            hv_ref[...] = jnp.zeros(hv_ref.shape, BF16)
        else:
            for d in range(1, N_SLOTS):
                r0 = LEFT_CTX - GROUP * d
                put(N_SLOTS - d, ck_ref[0, r0:r0 + GROUP, :], cv_ref[0, r0:r0 + GROUP, :])

    x = x_ref[0]
    h = _rms_norm(x, ng_ref[...]).astype(BF16)
    qkv = _dot(h, wqkv_ref[...])
    q = qkv[:, :D_MODEL]
    k = qkv[:, D_MODEL:2 * D_MODEL]
    v = qkv[:, 2 * D_MODEL:]
    qn = q * lax.rsqrt(_head_mean_sq(q, bd_ref) + EPS) * qg_ref[...]
    kn = k * lax.rsqrt(_head_mean_sq(k, bd_ref) + EPS) * kg_ref[...]
    ko_ref[0] = kn
    vo_ref[0] = v
    put(t % N_SLOTS, kn, v)
    qb = (qn * ATTN_SCALE).astype(BF16)

    half_lane = lax.broadcasted_iota(jnp.int32, (CHUNK, 2 * GROUP), 1) % GROUP
    second_key_chunk = half_lane >= CHUNK
    first_head = lax.broadcasted_iota(jnp.int32, (GROUP, LANES), 1) < HEAD_DIM

    o_cols = []
    for j in range(N_PAIRS):
        ls = slice(LANES * j, LANES * (j + 1))
        qp = qb[:, ls]
        s_even, s_odd, slots = [], [], []
        for d in range(N_SLOTS):
            slot = (t + N_SLOTS - d) % N_SLOTS
            slots.append(slot)
            s = _dot_nt(qp, hk_ref[slot, :, ls]) + bias_ref[d, j]
            if d == 0:
                s = jnp.concatenate(
                    [jnp.where(second_key_chunk, NEG_INF, s[:CHUNK]), s[CHUNK:]], axis=0)
            if d == N_SLOTS - 1:
                s = jnp.concatenate(
                    [s[:CHUNK], jnp.where(second_key_chunk, s[CHUNK:], NEG_INF)], axis=0)
            if fresh and d > 0:
                s = jnp.where(t >= d, s, NEG_INF)
            s_even.append(s[:, :GROUP])
            s_odd.append(s[:, GROUP:])
        p_halves, inv_l = [], []
        for tiles in (s_even, s_odd):
            m = functools.reduce(jnp.maximum, tiles)
            m = jnp.max(m, axis=-1, keepdims=True)
            p = [jnp.exp(s - m) for s in tiles]
            l = jnp.sum(functools.reduce(jnp.add, p), axis=-1, keepdims=True)
            p_halves.append(p)
            inv_l.append(1.0 / l)
        o = None
        for d in range(N_SLOTS):
            p = jnp.concatenate([p_halves[0][d], p_halves[1][d]], axis=1).astype(BF16)
            part = _dot(p, hv_ref[slots[d], :, ls])
            o = part if o is None else o + part
        o_cols.append(o * jnp.where(first_head, inv_l[0], inv_l[1]))
    o = jnp.concatenate(o_cols, axis=1).astype(BF16)
    xo_ref[0] = x + _dot(o, wo_ref[...])


def _rel_bias_tiles(table):
    r = jnp.arange(GROUP)[:, None]
    c = jnp.arange(2 * GROUP)[None, :]
    d = jnp.arange(N_SLOTS)[:, None, None]
    dist = r - (c % GROUP) + GROUP * d
    idx = jnp.clip(dist, -REL_CLIP, REL_CLIP) + REL_CLIP
    head = 2 * jnp.arange(N_PAIRS)[:, None, None] + (c // GROUP)
    return table[idx[:, None], head[None]].astype(F32)


def _attention(x, cache_k, cache_v, norm_g, w_qkv, q_gain, k_gain, rel_table, w_o):
    b, t, d = x.shape
    assert t % GROUP == 0
    n_t = t // GROUP
    keep = min(N_SLOTS - 1, n_t)
    fresh = cache_k is None
    bd = jnp.kron(jnp.eye(MXU_DIM // HEAD_DIM, dtype=F32),
                  jnp.full((HEAD_DIM, HEAD_DIM), 1.0 / HEAD_DIM, F32)).astype(BF16)
    x_spec = pl.BlockSpec((1, GROUP, d), lambda i, j: (i, j, 0))
    cache_spec = pl.BlockSpec((1, LEFT_CTX, d), lambda i, j: (i, 0, 0))
    kv_spec = pl.BlockSpec((1, GROUP, d), lambda i, j: (i, jnp.maximum(j - (n_t - keep), 0), 0))
    in_specs = [x_spec] + ([] if fresh else [cache_spec, cache_spec]) + [
        _const_spec((1, d)),
        _const_spec((d, 3 * d)),
        _const_spec((1, d)),
        _const_spec((1, d)),
        _const_spec((N_SLOTS, N_PAIRS, GROUP, 2 * GROUP)),
        _const_spec((d, d)),
        _const_spec((MXU_DIM, MXU_DIM)),
    ]
    args = [x] + ([] if fresh else [cache_k, cache_v]) + [
        norm_g.reshape(1, d), w_qkv,
        jnp.tile(q_gain, N_HEADS).reshape(1, d), jnp.tile(k_gain, N_HEADS).reshape(1, d),
        _rel_bias_tiles(rel_table), w_o, bd,
    ]
    kv_shape = jax.ShapeDtypeStruct((b, keep * GROUP, d), F32)
    return pl.pallas_call(
        functools.partial(_attn_body, fresh=fresh),
        grid=(b, n_t),
        in_specs=in_specs,
        out_specs=[x_spec, kv_spec, kv_spec],
        out_shape=[jax.ShapeDtypeStruct((b, t, d), F32), kv_shape, kv_shape],
        scratch_shapes=[pltpu.VMEM((N_SLOTS, 2 * GROUP, d), BF16),
                        pltpu.VMEM((N_SLOTS, 2 * GROUP, d), BF16)],
        compiler_params=pltpu.CompilerParams(
            dimension_semantics=("arbitrary", "arbitrary"), vmem_limit_bytes=VMEM_LIMIT_BYTES),
        name="attn_fresh" if fresh else "attn_cached",
    )(*args)


def _conv_body(x_ref, st_ref, ng_ref, w1_ref, b1_ref, wdw_ref, bdw_ref, lng_ref, lnb_ref,
               w2_ref, b2_ref, xo_ref, so_ref, ext_ref, y_ref, *, tm):
    t = pl.program_id(1)
    first = CONV_PAD - CONV_HIST

    @pl.when(t == 0)
    def _load_state():
        ext_ref[0:first, :] = jnp.zeros((first, D_MODEL), F32)
        ext_ref[first:CONV_PAD, :] = st_ref[0]

    @pl.when(t > 0)
    def _carry_state():
        ext_ref[first:CONV_PAD, :] = ext_ref[tm + first:tm + CONV_PAD, :]

    x = x_ref[0]
    h = _rms_norm(x, ng_ref[...]).astype(BF16)
    ag = _dot(h, w1_ref[...]) + b1_ref[...]
    ext_ref[CONV_PAD:CONV_PAD + tm, :] = ag[:, :D_MODEL] * jax.nn.sigmoid(ag[:, D_MODEL:])
    so_ref[0] = ext_ref[tm + first:tm + CONV_PAD, :]

    for r0 in range(0, tm, CONV_ROWS):
        for c0 in range(0, D_MODEL, LANES):
            acc = jnp.broadcast_to(bdw_ref[:, c0:c0 + LANES], (CONV_ROWS, LANES))
            for j in range(CONV_WIDTH):
                rows = slice(r0 + first + j, r0 + first + j + CONV_ROWS)
                acc = acc + ext_ref[rows, c0:c0 + LANES] * wdw_ref[j:j + 1, c0:c0 + LANES]
            y_ref[r0:r0 + CONV_ROWS, c0:c0 + LANES] = acc

    y = y_ref[...]
    mu = jnp.mean(y, axis=-1, keepdims=True)
    yc = y - mu
    var = jnp.mean(yc * yc, axis=-1, keepdims=True)
    yn = yc * lax.rsqrt(var + EPS) * lng_ref[...] + lnb_ref[...]
    z = (yn * jax.nn.sigmoid(yn)).astype(BF16)
    xo_ref[0] = x + _dot(z, w2_ref[...]) + b2_ref[...]


def _conv_module(x, state, norm_g, w_pw1, b_pw1, w_dw, b_dw, ln_g, ln_b, w_pw2, b_pw2, tm):
    b, t, d = x.shape
    assert t % tm == 0 and tm % CONV_ROWS == 0 and tm >= CONV_HIST
    x_spec = pl.BlockSpec((1, tm, d), lambda i, j: (i, j, 0))
    st_spec = pl.BlockSpec((1, CONV_HIST, d), lambda i, j: (i, 0, 0))
    return pl.pallas_call(
        functools.partial(_conv_body, tm=tm),
        grid=(b, t // tm),
        in_specs=[
            x_spec, st_spec,
            _const_spec((1, d)),
            _const_spec((d, 2 * d)),
            _const_spec((1, 2 * d)),
            _const_spec((CONV_WIDTH, d)),
            _const_spec((1, d)),
            _const_spec((1, d)),
            _const_spec((1, d)),
            _const_spec((d, d)),
            _const_spec((1, d)),
        ],
        out_specs=[x_spec, st_spec],
        out_shape=[jax.ShapeDtypeStruct((b, t, d), F32),
                   jax.ShapeDtypeStruct((b, CONV_HIST, d), F32)],
        scratch_shapes=[pltpu.VMEM((CONV_PAD + tm, d), F32), pltpu.VMEM((tm, d), F32)],
        compiler_params=pltpu.CompilerParams(
            dimension_semantics=("arbitrary", "arbitrary"), vmem_limit_bytes=VMEM_LIMIT_BYTES),
        name="conv",
    )(x, state, norm_g.reshape(1, d), w_pw1, b_pw1.reshape(1, 2 * d), w_dw, b_dw.reshape(1, d),
      ln_g.reshape(1, d), ln_b.reshape(1, d), w_pw2, b_pw2.reshape(1, d))


def _trunk(x, cache_k, cache_v, conv_state, p, conv_tile):
    seq = x.shape[1]
    x = _ffn(x, p['ffn1_norm'][0], p['ffn1_w_gate'][0], p['ffn1_w_up'][0], p['ffn1_w_down'][0])
    pad = (-seq) % GROUP
    xa = jnp.pad(x, ((0, 0), (0, pad), (0, 0))) if pad else x
    xa, k_new, v_new = _attention(
        xa, cache_k, cache_v, p['attn_norm'][0], p['attn_w_qkv'][0], p['attn_q_gain'][0],
        p['attn_k_gain'][0], p['attn_rel_bias'][0], p['attn_w_o'][0])
    x = xa[:, :seq] if pad else xa
    x = _ffn(x, p['ffn2_norm'][0], p['ffn2_w_gate'][0], p['ffn2_w_up'][0], p['ffn2_w_down'][0])
    x = _ffn(x, p['ffn1_norm'][1], p['ffn1_w_gate'][1], p['ffn1_w_up'][1], p['ffn1_w_down'][1])
    x, conv_new = _conv_module(
        x, conv_state, p['conv_norm'][0], p['conv_w_pw1'][0], p['conv_b_pw1'][0],
        p['conv_w_dw'][0], p['conv_b_dw'][0], p['conv_ln_g'][0], p['conv_ln_b'][0],
        p['conv_w_pw2'][0], p['conv_b_pw2'][0], conv_tile)
    x = _ffn(x, p['ffn2_norm'][1], p['ffn2_w_gate'][1], p['ffn2_w_up'][1], p['ffn2_w_down'][1])
    return x, k_new, v_new, conv_new


def kernel(x_prompt, x_sample, cache_attn_k, cache_attn_v, state_conv, ffn1_norm, ffn1_w_gate, ffn1_w_up, ffn1_w_down, ffn2_norm, ffn2_w_gate, ffn2_w_up, ffn2_w_down, attn_norm, attn_w_qkv, attn_q_gain, attn_k_gain, attn_rel_bias, attn_w_o, conv_norm, conv_w_pw1, conv_b_pw1, conv_w_dw, conv_b_dw, conv_ln_g, conv_ln_b, conv_w_pw2, conv_b_pw2):
    p = dict(
        ffn1_norm=ffn1_norm, ffn2_norm=ffn2_norm, attn_norm=attn_norm, attn_q_gain=attn_q_gain,
        attn_k_gain=attn_k_gain, attn_rel_bias=attn_rel_bias, conv_norm=conv_norm,
        conv_b_pw1=conv_b_pw1, conv_w_dw=conv_w_dw, conv_b_dw=conv_b_dw, conv_ln_g=conv_ln_g,
        conv_ln_b=conv_ln_b, conv_b_pw2=conv_b_pw2,
        ffn1_w_gate=ffn1_w_gate.astype(BF16), ffn1_w_up=ffn1_w_up.astype(BF16),
        ffn1_w_down=ffn1_w_down.astype(BF16), ffn2_w_gate=ffn2_w_gate.astype(BF16),
        ffn2_w_up=ffn2_w_up.astype(BF16), ffn2_w_down=ffn2_w_down.astype(BF16),
        attn_w_qkv=attn_w_qkv.astype(BF16), attn_w_o=attn_w_o.astype(BF16),
        conv_w_pw1=conv_w_pw1.astype(BF16), conv_w_pw2=conv_w_pw2.astype(BF16))

    bp, sp, d = x_prompt.shape
    bs, ss, _ = x_sample.shape
    assert attn_w_qkv.shape[0] == 1 and conv_w_pw1.shape[0] == 1

    zero_conv = jnp.zeros((bp, CONV_HIST, d), F32)
    y_p, k_p, v_p, conv_p = _trunk(x_prompt, None, None, zero_conv, p, conv_tile=256)
    keep = min(LEFT_CTX, sp)
    k_p = k_p[:, k_p.shape[1] - keep:].reshape(1, bp, keep, N_HEADS, HEAD_DIM)
    v_p = v_p[:, v_p.shape[1] - keep:].reshape(1, bp, keep, N_HEADS, HEAD_DIM)

    y_s, k_s, v_s, conv_s = _trunk(
        x_sample, cache_attn_k[0].reshape(bs, LEFT_CTX, d), cache_attn_v[0].reshape(bs, LEFT_CTX, d),
        state_conv[0], p, conv_tile=ss)
    k_s = k_s[:, :ss].reshape(1, bs, ss, N_HEADS, HEAD_DIM)
    v_s = v_s[:, :ss].reshape(1, bs, ss, N_HEADS, HEAD_DIM)
    return (y_p, y_s, k_p, v_p, k_s, v_s, conv_p[None], conv_s[None])
```

```python
import functools

import jax
import jax.numpy as jnp
from jax import lax
from jax.experimental import pallas as pl
from jax.experimental.pallas import tpu as pltpu

D_MODEL = 1024
N_HEADS = 16
HEAD_DIM = D_MODEL // N_HEADS
CHUNK = 64
LEFT_CHUNKS = 8
LEFT_CTX = LEFT_CHUNKS * CHUNK
REL_CLIP = 128
CONV_WIDTH = 31
CONV_HIST = CONV_WIDTH - 1
D_FF = 2816
EPS = 1e-6
NEG_INF = -1e30

BF16 = jnp.bfloat16
F32 = jnp.float32

LANES = 128
MXU_DIM = 256
VMEM_LIMIT_BYTES = 56 * 1024 * 1024

GROUP = 2 * CHUNK
N_SLOTS = LEFT_CTX // GROUP + 1
N_PAIRS = N_HEADS // 2
ATTN_SCALE = HEAD_DIM ** -0.5

FFN_TILE = 512
FF_CHUNKS = ((0, 512), (512, 1024), (1024, 1536), (1536, 2048), (2048, 2560), (2560, 2816))
CONV_ROWS = 32
CONV_PAD = 32


def _dot(a, b):
    return jnp.dot(a, b, preferred_element_type=F32)


def _dot_nt(a, b):
    return lax.dot_general(a, b, (((1,), (1,)), ((), ())), preferred_element_type=F32)


def _rms_norm(x, g):
    ms = jnp.mean(x * x, axis=-1, keepdims=True)
    return x * lax.rsqrt(ms + EPS) * g


def _const_spec(shape):
    zeros = (0,) * len(shape)
    return pl.BlockSpec(shape, lambda *_: zeros, pipeline_mode=pl.Buffered(1))


def _ffn_body(x_ref, g_ref, wg_ref, wu_ref, wd_ref, o_ref):
    x = x_ref[...]
    h = _rms_norm(x, g_ref[...]).astype(BF16)
    acc = None
    for c0, c1 in FF_CHUNKS:
        gate = _dot(h, wg_ref[:, c0:c1])
        up = _dot(h, wu_ref[:, c0:c1])
        a = (gate * jax.nn.sigmoid(gate) * up).astype(BF16)
        part = _dot(a, wd_ref[c0:c1, :])
        acc = part if acc is None else acc + part
    o_ref[...] = x + 0.5 * acc


def _ffn(x, g, wg, wu, wd):
    b, t, d = x.shape
    rows = b * t
    tm = min(FFN_TILE, rows)
    assert rows % tm == 0
    out = pl.pallas_call(
        _ffn_body,
        grid=(rows // tm,),
        in_specs=[
            pl.BlockSpec((tm, d), lambda i: (i, 0)),
            _const_spec((1, d)),
            _const_spec((d, D_FF)),
            _const_spec((d, D_FF)),
            _const_spec((D_FF, d)),
        ],
        out_specs=pl.BlockSpec((tm, d), lambda i: (i, 0)),
        out_shape=jax.ShapeDtypeStruct((rows, d), F32),
        compiler_params=pltpu.CompilerParams(
            dimension_semantics=("arbitrary",), vmem_limit_bytes=VMEM_LIMIT_BYTES),
        name="ffn",
    )(x.reshape(rows, d), g.reshape(1, d), wg, wu, wd)
    return out.reshape(b, t, d)


def _head_mean_sq(y, bd_ref):
    sq = y * y
    hi = sq.astype(BF16)
    lo = (sq - hi.astype(F32)).astype(BF16)
    bd = bd_ref[...]
    cols = []
    for c in range(0, D_MODEL, MXU_DIM):
        cols.append(_dot(hi[:, c:c + MXU_DIM], bd) + _dot(lo[:, c:c + MXU_DIM], bd))
    return jnp.concatenate(cols, axis=1)


def _attn_body(*refs, fresh):
    if fresh:
        (x_ref, ng_ref, wqkv_ref, qg_ref, kg_ref, bias_ref, far_ref, wo_ref, bd_ref,
         xo_ref, ko_ref, vo_ref, hk_ref, hv_ref) = refs
        ck_ref = cv_ref = None
    else:
        (x_ref, ck_ref, cv_ref, ng_ref, wqkv_ref, qg_ref, kg_ref, bias_ref, far_ref, wo_ref,
         bd_ref, xo_ref, ko_ref, vo_ref, hk_ref, hv_ref) = refs
    t = pl.program_id(1)

    lane = lax.broadcasted_iota(jnp.int32, (GROUP, D_MODEL), 1)
    even_head = (lane % LANES) < HEAD_DIM

    def put(slot, k, v):
        for ref, val in ((hk_ref, k), (hv_ref, v)):
            ref[slot, 0:GROUP, :] = jnp.where(even_head, val, 0.0).astype(BF16)
            ref[slot, GROUP:2 * GROUP, :] = jnp.where(even_head, 0.0, val).astype(BF16)

    @pl.when(t == 0)
    def _init():
        if fresh:
            hk_ref[...] = jnp.zeros(hk_ref.shape, BF16)
            hv_ref[...] = jnp.zeros(hv_ref.shape, BF16)
        else:
            for d in range(1, N_SLOTS):
                r0 = LEFT_CTX - GROUP * d
                put(N_SLOTS - d, ck_ref[0, r0:r0 + GROUP, :], cv_ref[0, r0:r0 + GROUP, :])

    x = x_ref[0]
    h = _rms_norm(x, ng_ref[...]).astype(BF16)
    qkv = _dot(h, wqkv_ref[...])
    q = qkv[:, :D_MODEL]
    k = qkv[:, D_MODEL:2 * D_MODEL]
    v = qkv[:, 2 * D_MODEL:]
    qn = q * lax.rsqrt(_head_mean_sq(q, bd_ref) + EPS) * qg_ref[...]
    kn = k * lax.rsqrt(_head_mean_sq(k, bd_ref) + EPS) * kg_ref[...]
    ko_ref[0] = kn
    vo_ref[0] = v
    put(t % N_SLOTS, kn, v)
    qb = (qn * ATTN_SCALE).astype(BF16)

    half_lane = lax.broadcasted_iota(jnp.int32, (CHUNK, 2 * GROUP), 1) % GROUP
    second_key_chunk = half_lane >= CHUNK
    first_head = lax.broadcasted_iota(jnp.int32, (GROUP, LANES), 1) < HEAD_DIM

    o_cols = []
    for j in range(N_PAIRS):
        ls = slice(LANES * j, LANES * (j + 1))
        qp = qb[:, ls]
        s_even, s_odd, slots = [], [], []
        for d in range(N_SLOTS):
            slot = (t + N_SLOTS - d) % N_SLOTS
            slots.append(slot)
            s = _dot_nt(qp, hk_ref[slot, :, ls]) + (bias_ref[d, j] if d < 2 else far_ref[j])
            if d == 0:
                s = jnp.concatenate(
                    [jnp.where(second_key_chunk, NEG_INF, s[:CHUNK]), s[CHUNK:]], axis=0)
            if d == N_SLOTS - 1:
                s = jnp.concatenate(
                    [s[:CHUNK], jnp.where(second_key_chunk, s[CHUNK:], NEG_INF)], axis=0)
            if fresh and d > 0:
                s = jnp.where(t >= d, s, NEG_INF)
            s_even.append(s[:, :GROUP])
            s_odd.append(s[:, GROUP:])
        p_halves, inv_l = [], []
        for tiles in (s_even, s_odd):
            m = functools.reduce(jnp.maximum, tiles)
            m = jnp.max(m, axis=-1, keepdims=True)
            p = [jnp.exp(s - m) for s in tiles]
            l = jnp.sum(functools.reduce(jnp.add, p), axis=-1, keepdims=True)
            p_halves.append(p)
            inv_l.append(1.0 / l)
        o = None
        for d in range(N_SLOTS):
            p = jnp.concatenate([p_halves[0][d], p_halves[1][d]], axis=1).astype(BF16)
            part = _dot(p, hv_ref[slots[d], :, ls])
            o = part if o is None else o + part
        o_cols.append(o * jnp.where(first_head, inv_l[0], inv_l[1]))
    o = jnp.concatenate(o_cols, axis=1).astype(BF16)
    xo_ref[0] = x + _dot(o, wo_ref[...])


def _rel_bias_tiles(table):
    assert REL_CLIP == GROUP
    tt = table.T.astype(F32)
    last = tt[:, 2 * REL_CLIP:]
    near = jnp.concatenate([tt[:, REL_CLIP:0:-1], tt[:, 2 * REL_CLIP:REL_CLIP:-1]], axis=1)
    prev = jnp.concatenate([tt[:, 2 * REL_CLIP:REL_CLIP:-1],
                            jnp.broadcast_to(last, (N_HEADS, GROUP))], axis=1)

    def toeplitz(w):
        flat = jnp.tile(w, (1, GROUP))[:, :GROUP * (2 * GROUP - 1)]
        return flat.reshape(N_HEADS, GROUP, 2 * GROUP - 1)[:, :, :GROUP]

    tiles = jnp.stack([toeplitz(near), toeplitz(prev)])
    tiles = tiles.reshape(2, N_PAIRS, 2, GROUP, GROUP).transpose(0, 1, 3, 2, 4)
    far = jnp.repeat(last.reshape(N_PAIRS, 1, 2), GROUP, axis=2)
    return tiles.reshape(2, N_PAIRS, GROUP, 2 * GROUP), far


def _attention(x, cache_k, cache_v, norm_g, w_qkv, q_gain, k_gain, rel_table, w_o):
    b, t, d = x.shape
    assert t % GROUP == 0
    n_t = t // GROUP
    keep = min(N_SLOTS - 1, n_t)
    fresh = cache_k is None
    bd = jnp.kron(jnp.eye(MXU_DIM // HEAD_DIM, dtype=F32),
                  jnp.full((HEAD_DIM, HEAD_DIM), 1.0 / HEAD_DIM, F32)).astype(BF16)
    x_spec = pl.BlockSpec((1, GROUP, d), lambda i, j: (i, j, 0))
    cache_spec = pl.BlockSpec((1, LEFT_CTX, d), lambda i, j: (i, 0, 0))
    kv_spec = pl.BlockSpec((1, GROUP, d), lambda i, j: (i, jnp.maximum(j - (n_t - keep), 0), 0))
    in_specs = [x_spec] + ([] if fresh else [cache_spec, cache_spec]) + [
        _const_spec((1, d)),
        _const_spec((d, 3 * d)),
        _const_spec((1, d)),
        _const_spec((1, d)),
        _const_spec((2, N_PAIRS, GROUP, 2 * GROUP)),
        _const_spec((N_PAIRS, 1, 2 * GROUP)),
        _const_spec((d, d)),
        _const_spec((MXU_DIM, MXU_DIM)),
    ]
    args = [x] + ([] if fresh else [cache_k, cache_v]) + [
        norm_g.reshape(1, d), w_qkv,
        jnp.tile(q_gain, N_HEADS).reshape(1, d), jnp.tile(k_gain, N_HEADS).reshape(1, d),
        *_rel_bias_tiles(rel_table), w_o, bd,
    ]
    kv_shape = jax.ShapeDtypeStruct((b, keep * GROUP, d), F32)
    return pl.pallas_call(
        functools.partial(_attn_body, fresh=fresh),
        grid=(b, n_t),
        in_specs=in_specs,
        out_specs=[x_spec, kv_spec, kv_spec],
        out_shape=[jax.ShapeDtypeStruct((b, t, d), F32), kv_shape, kv_shape],
        scratch_shapes=[pltpu.VMEM((N_SLOTS, 2 * GROUP, d), BF16),
                        pltpu.VMEM((N_SLOTS, 2 * GROUP, d), BF16)],
        compiler_params=pltpu.CompilerParams(
            dimension_semantics=("arbitrary", "arbitrary"), vmem_limit_bytes=VMEM_LIMIT_BYTES),
        name="attn_fresh" if fresh else "attn_cached",
    )(*args)


def _conv_body(x_ref, st_ref, ng_ref, w1_ref, b1_ref, wdw_ref, bdw_ref, lng_ref, lnb_ref,
               w2_ref, b2_ref, xo_ref, so_ref, ext_ref, y_ref, *, tm):
    t = pl.program_id(1)
    first = CONV_PAD - CONV_HIST

    @pl.when(t == 0)
    def _load_state():
        ext_ref[0:first, :] = jnp.zeros((first, D_MODEL), F32)
        ext_ref[first:CONV_PAD, :] = st_ref[0]

    @pl.when(t > 0)
    def _carry_state():
        ext_ref[first:CONV_PAD, :] = ext_ref[tm + first:tm + CONV_PAD, :]

    x = x_ref[0]
    h = _rms_norm(x, ng_ref[...]).astype(BF16)
    ag = _dot(h, w1_ref[...]) + b1_ref[...]
    ext_ref[CONV_PAD:CONV_PAD + tm, :] = ag[:, :D_MODEL] * jax.nn.sigmoid(ag[:, D_MODEL:])
    so_ref[0] = ext_ref[tm + first:tm + CONV_PAD, :]

    for r0 in range(0, tm, CONV_ROWS):
        for c0 in range(0, D_MODEL, LANES):
            acc = jnp.broadcast_to(bdw_ref[:, c0:c0 + LANES], (CONV_ROWS, LANES))
            for j in range(CONV_WIDTH):
                rows = slice(r0 + first + j, r0 + first + j + CONV_ROWS)
                acc = acc + ext_ref[rows, c0:c0 + LANES] * wdw_ref[j:j + 1, c0:c0 + LANES]
            y_ref[r0:r0 + CONV_ROWS, c0:c0 + LANES] = acc

    y = y_ref[...]
    mu = jnp.mean(y, axis=-1, keepdims=True)
    yc = y - mu
    var = jnp.mean(yc * yc, axis=-1, keepdims=True)
    yn = yc * lax.rsqrt(var + EPS) * lng_ref[...] + lnb_ref[...]
    z = (yn * jax.nn.sigmoid(yn)).astype(BF16)
    xo_ref[0] = x + _dot(z, w2_ref[...]) + b2_ref[...]


def _conv_module(x, state, norm_g, w_pw1, b_pw1, w_dw, b_dw, ln_g, ln_b, w_pw2, b_pw2, tm):
    b, t, d = x.shape
    assert t % tm == 0 and tm % CONV_ROWS == 0 and tm >= CONV_HIST
    x_spec = pl.BlockSpec((1, tm, d), lambda i, j: (i, j, 0))
    st_spec = pl.BlockSpec((1, CONV_HIST, d), lambda i, j: (i, 0, 0))
    return pl.pallas_call(
        functools.partial(_conv_body, tm=tm),
        grid=(b, t // tm),
        in_specs=[
            x_spec, st_spec,
            _const_spec((1, d)),
            _const_spec((d, 2 * d)),
            _const_spec((1, 2 * d)),
            _const_spec((CONV_WIDTH, d)),
            _const_spec((1, d)),
            _const_spec((1, d)),
            _const_spec((1, d)),
            _const_spec((d, d)),
            _const_spec((1, d)),
        ],
        out_specs=[x_spec, st_spec],
        out_shape=[jax.ShapeDtypeStruct((b, t, d), F32),
                   jax.ShapeDtypeStruct((b, CONV_HIST, d), F32)],
        scratch_shapes=[pltpu.VMEM((CONV_PAD + tm, d), F32), pltpu.VMEM((tm, d), F32)],
        compiler_params=pltpu.CompilerParams(
            dimension_semantics=("arbitrary", "arbitrary"), vmem_limit_bytes=VMEM_LIMIT_BYTES),
        name="conv",
    )(x, state, norm_g.reshape(1, d), w_pw1, b_pw1.reshape(1, 2 * d), w_dw, b_dw.reshape(1, d),
      ln_g.reshape(1, d), ln_b.reshape(1, d), w_pw2, b_pw2.reshape(1, d))


def _trunk(x, cache_k, cache_v, conv_state, p, conv_tile):
    seq = x.shape[1]
    x = _ffn(x, p['ffn1_norm'][0], p['ffn1_w_gate'][0], p['ffn1_w_up'][0], p['ffn1_w_down'][0])
    pad = (-seq) % GROUP
    xa = jnp.pad(x, ((0, 0), (0, pad), (0, 0))) if pad else x
    xa, k_new, v_new = _attention(
        xa, cache_k, cache_v, p['attn_norm'][0], p['attn_w_qkv'][0], p['attn_q_gain'][0],
        p['attn_k_gain'][0], p['attn_rel_bias'][0], p['attn_w_o'][0])
    x = xa[:, :seq] if pad else xa
    x = _ffn(x, p['ffn2_norm'][0], p['ffn2_w_gate'][0], p['ffn2_w_up'][0], p['ffn2_w_down'][0])
    x = _ffn(x, p['ffn1_norm'][1], p['ffn1_w_gate'][1], p['ffn1_w_up'][1], p['ffn1_w_down'][1])
    x, conv_new = _conv_module(
        x, conv_state, p['conv_norm'][0], p['conv_w_pw1'][0], p['conv_b_pw1'][0],
        p['conv_w_dw'][0], p['conv_b_dw'][0], p['conv_ln_g'][0], p['conv_ln_b'][0],
        p['conv_w_pw2'][0], p['conv_b_pw2'][0], conv_tile)
    x = _ffn(x, p['ffn2_norm'][1], p['ffn2_w_gate'][1], p['ffn2_w_up'][1], p['ffn2_w_down'][1])
    return x, k_new, v_new, conv_new


def kernel(x_prompt, x_sample, cache_attn_k, cache_attn_v, state_conv, ffn1_norm, ffn1_w_gate, ffn1_w_up, ffn1_w_down, ffn2_norm, ffn2_w_gate, ffn2_w_up, ffn2_w_down, attn_norm, attn_w_qkv, attn_q_gain, attn_k_gain, attn_rel_bias, attn_w_o, conv_norm, conv_w_pw1, conv_b_pw1, conv_w_dw, conv_b_dw, conv_ln_g, conv_ln_b, conv_w_pw2, conv_b_pw2):
    p = dict(
        ffn1_norm=ffn1_norm, ffn2_norm=ffn2_norm, attn_norm=attn_norm, attn_q_gain=attn_q_gain,
        attn_k_gain=attn_k_gain, attn_rel_bias=attn_rel_bias, conv_norm=conv_norm,
        conv_b_pw1=conv_b_pw1, conv_w_dw=conv_w_dw, conv_b_dw=conv_b_dw, conv_ln_g=conv_ln_g,
        conv_ln_b=conv_ln_b, conv_b_pw2=conv_b_pw2,
        ffn1_w_gate=ffn1_w_gate.astype(BF16), ffn1_w_up=ffn1_w_up.astype(BF16),
        ffn1_w_down=ffn1_w_down.astype(BF16), ffn2_w_gate=ffn2_w_gate.astype(BF16),
        ffn2_w_up=ffn2_w_up.astype(BF16), ffn2_w_down=ffn2_w_down.astype(BF16),
        attn_w_qkv=attn_w_qkv.astype(BF16), attn_w_o=attn_w_o.astype(BF16),
        conv_w_pw1=conv_w_pw1.astype(BF16), conv_w_pw2=conv_w_pw2.astype(BF16))

    bp, sp, d = x_prompt.shape
    bs, ss, _ = x_sample.shape
    assert attn_w_qkv.shape[0] == 1 and conv_w_pw1.shape[0] == 1

    zero_conv = jnp.zeros((bp, CONV_HIST, d), F32)
    y_p, k_p, v_p, conv_p = _trunk(x_prompt, None, None, zero_conv, p, conv_tile=256)
    keep = min(LEFT_CTX, sp)
    k_p = k_p[:, k_p.shape[1] - keep:].reshape(1, bp, keep, N_HEADS, HEAD_DIM)
    v_p = v_p[:, v_p.shape[1] - keep:].reshape(1, bp, keep, N_HEADS, HEAD_DIM)

    y_s, k_s, v_s, conv_s = _trunk(
        x_sample, cache_attn_k[0].reshape(bs, LEFT_CTX, d), cache_attn_v[0].reshape(bs, LEFT_CTX, d),
        state_conv[0], p, conv_tile=ss)
    k_s = k_s[:, :ss].reshape(1, bs, ss, N_HEADS, HEAD_DIM)
    v_s = v_s[:, :ss].reshape(1, bs, ss, N_HEADS, HEAD_DIM)
    return (y_p, y_s, k_p, v_p, k_s, v_s, conv_p[None], conv_s[None])
```

```python
import functools

import jax
import jax.numpy as jnp
from jax import lax
from jax.experimental import pallas as pl
from jax.experimental.pallas import tpu as pltpu

D_MODEL = 1024
N_HEADS = 16
HEAD_DIM = D_MODEL // N_HEADS
CHUNK = 64
LEFT_CHUNKS = 8
LEFT_CTX = LEFT_CHUNKS * CHUNK
REL_CLIP = 128
CONV_WIDTH = 31
CONV_HIST = CONV_WIDTH - 1
D_FF = 2816
EPS = 1e-6
NEG_INF = -1e30

BF16 = jnp.bfloat16
F32 = jnp.float32

LANES = 128
SUBLANES = 8
MXU_DIM = 256
VMEM_LIMIT_BYTES = 56 * 1024 * 1024

GROUP = 2 * CHUNK
N_SLOTS = LEFT_CTX // GROUP + 1
N_PAIRS = N_HEADS // 2
ATTN_SCALE = HEAD_DIM ** -0.5
LOG2_E = 1.4426950408889634
ATTN_GROUPS_PER_STEP = 2

FFN_TILE = 512
FF_CHUNKS = ((0, 512), (512, 1024), (1024, 1536), (1536, 2048), (2048, 2560), (2560, 2816))
CONV_TILE = 512
CONV_ROWS = 128
CONV_PAD = 32


def _dot(a, b):
    return jnp.dot(a, b, preferred_element_type=F32)


def _rms_norm(x, g):
    ms = jnp.mean(x * x, axis=-1, keepdims=True)
    return x * lax.rsqrt(ms + EPS) * g


def _const_spec(shape):
    zeros = (0,) * len(shape)
    return pl.BlockSpec(shape, lambda *_: zeros, pipeline_mode=pl.Buffered(1))


def _ffn_body(x_ref, g_ref, wg_ref, wu_ref, wd_ref, o_ref):
    x = x_ref[...]
    h = _rms_norm(x, g_ref[...]).astype(BF16)
    acc = None
    for c0, c1 in FF_CHUNKS:
        gate = _dot(h, wg_ref[:, c0:c1])
        up = _dot(h, wu_ref[:, c0:c1])
        a = (gate * jax.nn.sigmoid(gate) * up).astype(BF16)
        part = _dot(a, wd_ref[c0:c1, :])
        acc = part if acc is None else acc + part
    o_ref[...] = x + 0.5 * acc


def _ffn(x, g, wg, wu, wd):
    b, t, d = x.shape
    rows = b * t
    tm = min(FFN_TILE, rows)
    assert rows % tm == 0
    out = pl.pallas_call(
        _ffn_body,
        grid=(rows // tm,),
        in_specs=[
            pl.BlockSpec((tm, d), lambda i: (i, 0)),
            _const_spec((1, d)),
            _const_spec((d, D_FF)),
            _const_spec((d, D_FF)),
            _const_spec((D_FF, d)),
        ],
        out_specs=pl.BlockSpec((tm, d), lambda i: (i, 0)),
        out_shape=jax.ShapeDtypeStruct((rows, d), F32),
        compiler_params=pltpu.CompilerParams(
            dimension_semantics=("arbitrary",), vmem_limit_bytes=VMEM_LIMIT_BYTES),
        name="ffn",
    )(x.reshape(rows, d), g.reshape(1, d), wg, wu, wd)
    return out.reshape(b, t, d)


def _head_mean_sq(y, bd_ref):
    sq = y * y
    hi = sq.astype(BF16)
    lo = (sq - hi.astype(F32)).astype(BF16)
    bd = bd_ref[...]
    cols = []
    for c in range(0, D_MODEL, MXU_DIM):
        cols.append(_dot(hi[:, c:c + MXU_DIM], bd) + _dot(lo[:, c:c + MXU_DIM], bd))
    return jnp.concatenate(cols, axis=1)


def _attn_body(*refs, fresh, gps):
    if fresh:
        (x_ref, ng_ref, wqkv_ref, qg_ref, kg_ref, bias_ref, far_ref, wo_ref, bd_ref,
         xo_ref, ko_ref, vo_ref, hk_ref, hv_ref) = refs
        ck_ref = cv_ref = None
    else:
        (x_ref, ck_ref, cv_ref, ng_ref, wqkv_ref, qg_ref, kg_ref, bias_ref, far_ref, wo_ref,
         bd_ref, xo_ref, ko_ref, vo_ref, hk_ref, hv_ref) = refs
    t = pl.program_id(1)
    ring = N_SLOTS - 1 + gps

    lane = lax.broadcasted_iota(jnp.int32, (GROUP, D_MODEL), 1)
    even_head = (lane % LANES) < HEAD_DIM
    dim = lax.broadcasted_iota(jnp.int32, (D_MODEL, GROUP), 0)
    even_dim = (dim % LANES) < HEAD_DIM

    def put(slot, k, v):
        hv_ref[slot, 0:GROUP, :] = jnp.where(even_head, v, 0.0).astype(BF16)
        hv_ref[slot, GROUP:2 * GROUP, :] = jnp.where(even_head, 0.0, v).astype(BF16)
        kt = k.T
        hk_ref[slot, :, 0:GROUP] = jnp.where(even_dim, kt, 0.0).astype(BF16)
        hk_ref[slot, :, GROUP:2 * GROUP] = jnp.where(even_dim, 0.0, kt).astype(BF16)

    @pl.when(t == 0)
    def _init():
        if fresh:
            hk_ref[...] = jnp.zeros(hk_ref.shape, BF16)
            hv_ref[...] = jnp.zeros(hv_ref.shape, BF16)
        else:
            for d in range(1, N_SLOTS):
                r0 = LEFT_CTX - GROUP * d
                put(ring - d, ck_ref[0, r0:r0 + GROUP, :], cv_ref[0, r0:r0 + GROUP, :])

    x = x_ref[0]
    h = _rms_norm(x, ng_ref[...]).astype(BF16)
    qkv = _dot(h, wqkv_ref[...])
    q = qkv[:, :D_MODEL]
    k = qkv[:, D_MODEL:2 * D_MODEL]
    v = qkv[:, 2 * D_MODEL:]
    qn = q * lax.rsqrt(_head_mean_sq(q, bd_ref) + EPS) * qg_ref[...]
    kn = k * lax.rsqrt(_head_mean_sq(k, bd_ref) + EPS) * kg_ref[...]
    ko_ref[0] = kn
    vo_ref[0] = v
    for i in range(gps):
        rows = slice(GROUP * i, GROUP * (i + 1))
        put((t * gps + i) % ring, kn[rows], v[rows])
    qb = (qn * (ATTN_SCALE * LOG2_E)).astype(BF16)

    half_lane = lax.broadcasted_iota(jnp.int32, (CHUNK, 2 * GROUP), 1) % GROUP
    second_key_chunk = half_lane >= CHUNK
    key_row = lax.broadcasted_iota(jnp.int32, (2 * GROUP, LANES), 0)
    out_lane = lax.broadcasted_iota(jnp.int32, (2 * GROUP, LANES), 1)
    head_rows = ((key_row < GROUP) == (out_lane < HEAD_DIM)).astype(BF16)

    slots = [[(t * gps + i + ring - d) % ring for d in range(N_SLOTS)] for i in range(gps)]

    def scores(i, j):
        ls = slice(LANES * j, LANES * (j + 1))
        qp = qb[GROUP * i:GROUP * (i + 1), ls]
        s_even, s_odd = [], []
        for d in range(N_SLOTS):
            if d < 2:
                bias = bias_ref[d, j]
            elif fresh:
                bias = jnp.where(t * gps + i >= d, far_ref[j], NEG_INF)
            else:
                bias = far_ref[j]
            s = _dot(qp, hk_ref[slots[i][d], ls, :]) + bias
            if d == 0:
                s = jnp.concatenate(
                    [jnp.where(second_key_chunk, NEG_INF, s[:CHUNK]), s[CHUNK:]], axis=0)
            if d == N_SLOTS - 1:
                s = jnp.concatenate(
                    [s[:CHUNK], jnp.where(second_key_chunk, s[CHUNK:], NEG_INF)], axis=0)
            if fresh and d == 1:
                s = jnp.where(t * gps + i >= d, s, NEG_INF)
            s_even.append(s[:, :GROUP])
            s_odd.append(s[:, GROUP:])
        return s_even, s_odd

    def attend(i, j, s_even, s_odd):
        ls = slice(LANES * j, LANES * (j + 1))
        p_halves = []
        for tiles in (s_even, s_odd):
            m = functools.reduce(jnp.maximum, tiles)
            m = jnp.max(m, axis=-1, keepdims=True)
            p_halves.append([jnp.exp2(s - m) for s in tiles])
        o = None
        for d in range(N_SLOTS):
            p = jnp.concatenate([p_halves[0][d], p_halves[1][d]], axis=1).astype(BF16)
            part = _dot(p, jnp.concatenate([hv_ref[slots[i][d], :, ls], head_rows], axis=1))
            o = part if o is None else o + part
        return o[:, :LANES] / o[:, LANES:]

    work = [(i, j) for i in range(gps) for j in range(N_PAIRS)]
    o_cols = [[] for _ in range(gps)]
    pending = scores(*work[0])
    for n, (i, j) in enumerate(work):
        nxt = scores(*work[n + 1]) if n + 1 < len(work) else None
        o_cols[i].append(attend(i, j, *pending))
        pending = nxt
    o = jnp.concatenate([jnp.concatenate(c, axis=1) for c in o_cols], axis=0).astype(BF16)
    xo_ref[0] = x + _dot(o, wo_ref[...])


def _rel_bias_tiles(table):
    assert REL_CLIP == GROUP
    tt = table.T.astype(F32) * LOG2_E
    last = tt[:, 2 * REL_CLIP:]
    near = jnp.concatenate([tt[:, REL_CLIP:0:-1], tt[:, 2 * REL_CLIP:REL_CLIP:-1]], axis=1)
    prev = jnp.concatenate([tt[:, 2 * REL_CLIP:REL_CLIP:-1],
                            jnp.broadcast_to(last, (N_HEADS, GROUP))], axis=1)

    def toeplitz(w):
        flat = jnp.tile(w, (1, GROUP))[:, :GROUP * (2 * GROUP - 1)]
        return flat.reshape(N_HEADS, GROUP, 2 * GROUP - 1)[:, :, :GROUP]

    tiles = jnp.stack([toeplitz(near), toeplitz(prev)])
    tiles = tiles.reshape(2, N_PAIRS, 2, GROUP, GROUP).transpose(0, 1, 3, 2, 4)
    far = jnp.repeat(last.reshape(N_PAIRS, 1, 2), GROUP, axis=2)
    return tiles.reshape(2, N_PAIRS, GROUP, 2 * GROUP), far


def _attention(x, cache_k, cache_v, norm_g, w_qkv, q_gain, k_gain, rel_table, w_o):
    b, t, d = x.shape
    gps = ATTN_GROUPS_PER_STEP if t % (ATTN_GROUPS_PER_STEP * GROUP) == 0 else 1
    tq = gps * GROUP
    assert t % tq == 0 and (N_SLOTS - 1) % gps == 0
    n_t = t // tq
    keep = min((N_SLOTS - 1) // gps, n_t)
    fresh = cache_k is None
    bd = jnp.kron(jnp.eye(MXU_DIM // HEAD_DIM, dtype=F32),
                  jnp.full((HEAD_DIM, HEAD_DIM), 1.0 / HEAD_DIM, F32)).astype(BF16)
    x_spec = pl.BlockSpec((1, tq, d), lambda i, j: (i, j, 0))
    cache_spec = pl.BlockSpec((1, LEFT_CTX, d), lambda i, j: (i, 0, 0))
    kv_spec = pl.BlockSpec((1, tq, d), lambda i, j: (i, jnp.maximum(j - (n_t - keep), 0), 0))
    in_specs = [x_spec] + ([] if fresh else [cache_spec, cache_spec]) + [
        _const_spec((1, d)),
        _const_spec((d, 3 * d)),
        _const_spec((1, d)),
        _const_spec((1, d)),
        _const_spec((2, N_PAIRS, GROUP, 2 * GROUP)),
        _const_spec((N_PAIRS, 1, 2 * GROUP)),
        _const_spec((d, d)),
        _const_spec((MXU_DIM, MXU_DIM)),
    ]
    args = [x] + ([] if fresh else [cache_k, cache_v]) + [
        norm_g.reshape(1, d), w_qkv,
        jnp.tile(q_gain, N_HEADS).reshape(1, d), jnp.tile(k_gain, N_HEADS).reshape(1, d),
        *_rel_bias_tiles(rel_table), w_o, bd,
    ]
    kv_shape = jax.ShapeDtypeStruct((b, keep * tq, d), F32)
    return pl.pallas_call(
        functools.partial(_attn_body, fresh=fresh, gps=gps),
        grid=(b, n_t),
        in_specs=in_specs,
        out_specs=[x_spec, kv_spec, kv_spec],
        out_shape=[jax.ShapeDtypeStruct((b, t, d), F32), kv_shape, kv_shape],
        scratch_shapes=[pltpu.VMEM((N_SLOTS - 1 + gps, d, 2 * GROUP), BF16),
                        pltpu.VMEM((N_SLOTS - 1 + gps, 2 * GROUP, d), BF16)],
        compiler_params=pltpu.CompilerParams(
            dimension_semantics=("arbitrary", "arbitrary"), vmem_limit_bytes=VMEM_LIMIT_BYTES),
        name="attn_fresh" if fresh else "attn_cached",
    )(*args)


def _conv_body(x_ref, st_ref, ng_ref, w1_ref, b1_ref, wdw_ref, bdw_ref, lng_ref, lnb_ref,
               w2_ref, b2_ref, xo_ref, so_ref, ext_ref, y_ref, *, tm):
    t = pl.program_id(1)
    first = CONV_PAD - CONV_HIST

    @pl.when(t == 0)
    def _load_state():
        ext_ref[0:first, :] = jnp.zeros((first, D_MODEL), F32)
        ext_ref[first:CONV_PAD, :] = st_ref[0]

    @pl.when(t > 0)
    def _carry_state():
        ext_ref[first:CONV_PAD, :] = ext_ref[tm + first:tm + CONV_PAD, :]

    x = x_ref[0]
    h = _rms_norm(x, ng_ref[...]).astype(BF16)
    ag = _dot(h, w1_ref[...]) + b1_ref[...]
    ext_ref[CONV_PAD:CONV_PAD + tm, :] = ag[:, :D_MODEL] * jax.nn.sigmoid(ag[:, D_MODEL:])
    so_ref[0] = ext_ref[tm + first:tm + CONV_PAD, :]

    rows = min(CONV_ROWS, tm)
    for r0 in range(0, tm, rows):
        for c0 in range(0, D_MODEL, LANES):
            cols = slice(c0, c0 + LANES)
            out = jnp.broadcast_to(bdw_ref[:, cols], (rows, LANES))
            for res in range(SUBLANES):
                n = rows + (SUBLANES if res else 0)
                acc = None
                for j in range(CONV_WIDTH):
                    if (first + j) % SUBLANES != res:
                        continue
                    base = r0 + first + j - res
                    term = ext_ref[base:base + n, cols] * wdw_ref[j:j + 1, cols]
                    acc = term if acc is None else acc + term
                out = out + acc[res:res + rows]
            y_ref[r0:r0 + rows, cols] = out

    y = y_ref[...]
    mu = jnp.mean(y, axis=-1, keepdims=True)
    yc = y - mu
    var = jnp.mean(yc * yc, axis=-1, keepdims=True)
    yn = yc * lax.rsqrt(var + EPS) * lng_ref[...] + lnb_ref[...]
    z = (yn * jax.nn.sigmoid(yn)).astype(BF16)
    xo_ref[0] = x + _dot(z, w2_ref[...]) + b2_ref[...]


def _conv_module(x, state, norm_g, w_pw1, b_pw1, w_dw, b_dw, ln_g, ln_b, w_pw2, b_pw2):
    b, t, d = x.shape
    tm = min(CONV_TILE, t)
    assert t % tm == 0 and tm % min(CONV_ROWS, tm) == 0 and tm >= CONV_HIST
    x_spec = pl.BlockSpec((1, tm, d), lambda i, j: (i, j, 0))
    st_spec = pl.BlockSpec((1, CONV_HIST, d), lambda i, j: (i, 0, 0))
    return pl.pallas_call(
        functools.partial(_conv_body, tm=tm),
        grid=(b, t // tm),
        in_specs=[
            x_spec, st_spec,
            _const_spec((1, d)),
            _const_spec((d, 2 * d)),
            _const_spec((1, 2 * d)),
            _const_spec((CONV_WIDTH, d)),
            _const_spec((1, d)),
            _const_spec((1, d)),
            _const_spec((1, d)),
            _const_spec((d, d)),
            _const_spec((1, d)),
        ],
        out_specs=[x_spec, st_spec],
        out_shape=[jax.ShapeDtypeStruct((b, t, d), F32),
                   jax.ShapeDtypeStruct((b, CONV_HIST, d), F32)],
        scratch_shapes=[pltpu.VMEM((CONV_PAD + tm, d), F32), pltpu.VMEM((tm, d), F32)],
        compiler_params=pltpu.CompilerParams(
            dimension_semantics=("arbitrary", "arbitrary"), vmem_limit_bytes=VMEM_LIMIT_BYTES),
        name="conv",
    )(x, state, norm_g.reshape(1, d), w_pw1, b_pw1.reshape(1, 2 * d), w_dw, b_dw.reshape(1, d),
      ln_g.reshape(1, d), ln_b.reshape(1, d), w_pw2, b_pw2.reshape(1, d))


def _trunk(x, cache_k, cache_v, conv_state, p):
    seq = x.shape[1]
    x = _ffn(x, p['ffn1_norm'][0], p['ffn1_w_gate'][0], p['ffn1_w_up'][0], p['ffn1_w_down'][0])
    pad = (-seq) % GROUP
    xa = jnp.pad(x, ((0, 0), (0, pad), (0, 0))) if pad else x
    xa, k_new, v_new = _attention(
        xa, cache_k, cache_v, p['attn_norm'][0], p['attn_w_qkv'][0], p['attn_q_gain'][0],
        p['attn_k_gain'][0], p['attn_rel_bias'][0], p['attn_w_o'][0])
    x = xa[:, :seq] if pad else xa
    x = _ffn(x, p['ffn2_norm'][0], p['ffn2_w_gate'][0], p['ffn2_w_up'][0], p['ffn2_w_down'][0])
    x = _ffn(x, p['ffn1_norm'][1], p['ffn1_w_gate'][1], p['ffn1_w_up'][1], p['ffn1_w_down'][1])
    x, conv_new = _conv_module(
        x, conv_state, p['conv_norm'][0], p['conv_w_pw1'][0], p['conv_b_pw1'][0],
        p['conv_w_dw'][0], p['conv_b_dw'][0], p['conv_ln_g'][0], p['conv_ln_b'][0],
        p['conv_w_pw2'][0], p['conv_b_pw2'][0])
    x = _ffn(x, p['ffn2_norm'][1], p['ffn2_w_gate'][1], p['ffn2_w_up'][1], p['ffn2_w_down'][1])
    return x, k_new, v_new, conv_new


def kernel(x_prompt, x_sample, cache_attn_k, cache_attn_v, state_conv, ffn1_norm, ffn1_w_gate, ffn1_w_up, ffn1_w_down, ffn2_norm, ffn2_w_gate, ffn2_w_up, ffn2_w_down, attn_norm, attn_w_qkv, attn_q_gain, attn_k_gain, attn_rel_bias, attn_w_o, conv_norm, conv_w_pw1, conv_b_pw1, conv_w_dw, conv_b_dw, conv_ln_g, conv_ln_b, conv_w_pw2, conv_b_pw2):
    p = dict(
        ffn1_norm=ffn1_norm, ffn2_norm=ffn2_norm, attn_norm=attn_norm, attn_q_gain=attn_q_gain,
        attn_k_gain=attn_k_gain, attn_rel_bias=attn_rel_bias, conv_norm=conv_norm,
        conv_b_pw1=conv_b_pw1, conv_w_dw=conv_w_dw, conv_b_dw=conv_b_dw, conv_ln_g=conv_ln_g,
        conv_ln_b=conv_ln_b, conv_b_pw2=conv_b_pw2,
        ffn1_w_gate=ffn1_w_gate.astype(BF16), ffn1_w_up=ffn1_w_up.astype(BF16),
        ffn1_w_down=ffn1_w_down.astype(BF16), ffn2_w_gate=ffn2_w_gate.astype(BF16),
        ffn2_w_up=ffn2_w_up.astype(BF16), ffn2_w_down=ffn2_w_down.astype(BF16),
        attn_w_qkv=attn_w_qkv.astype(BF16), attn_w_o=attn_w_o.astype(BF16),
        conv_w_pw1=conv_w_pw1.astype(BF16), conv_w_pw2=conv_w_pw2.astype(BF16))

    bp, sp, d = x_prompt.shape
    bs, ss, _ = x_sample.shape
    assert attn_w_qkv.shape[0] == 1 and conv_w_pw1.shape[0] == 1

    zero_conv = jnp.zeros((bp, CONV_HIST, d), F32)
    y_p, k_p, v_p, conv_p = _trunk(x_prompt, None, None, zero_conv, p)
    keep = min(LEFT_CTX, sp)
    k_p = k_p[:, k_p.shape[1] - keep:].reshape(1, bp, keep, N_HEADS, HEAD_DIM)
    v_p = v_p[:, v_p.shape[1] - keep:].reshape(1, bp, keep, N_HEADS, HEAD_DIM)

    y_s, k_s, v_s, conv_s = _trunk(
        x_sample, cache_attn_k[0].reshape(bs, LEFT_CTX, d), cache_attn_v[0].reshape(bs, LEFT_CTX, d),
        state_conv[0], p)
    k_s = k_s[:, :ss].reshape(1, bs, ss, N_HEADS, HEAD_DIM)
    v_s = v_s[:, :ss].reshape(1, bs, ss, N_HEADS, HEAD_DIM)
    return (y_p, y_s, k_p, v_p, k_s, v_s, conv_p[None], conv_s[None])
```

```python
import functools

import jax
import jax.numpy as jnp
from jax import lax
from jax.experimental import pallas as pl
from jax.experimental.pallas import tpu as pltpu

D_MODEL = 1024
N_HEADS = 16
HEAD_DIM = D_MODEL // N_HEADS
CHUNK = 64
LEFT_CHUNKS = 8
LEFT_CTX = LEFT_CHUNKS * CHUNK
REL_CLIP = 128
CONV_WIDTH = 31
CONV_HIST = CONV_WIDTH - 1
D_FF = 2816
EPS = 1e-6
NEG_INF = -1e30

BF16 = jnp.bfloat16
F32 = jnp.float32

LANES = 128
SUBLANES = 8
MXU_DIM = 256
VMEM_LIMIT_BYTES = 56 * 1024 * 1024

GROUP = 2 * CHUNK
N_SLOTS = LEFT_CTX // GROUP + 1
N_PAIRS = N_HEADS // 2
ATTN_SCALE = HEAD_DIM ** -0.5
LOG2_E = 1.4426950408889634
ATTN_GROUPS_PER_STEP = 4

FFN_TILE = 512
FF_CHUNKS = ((0, 512), (512, 1024), (1024, 1536), (1536, 2048), (2048, 2560), (2560, 2816))
CONV_TILE = 512
FUSED_FF_CHUNKS = tuple((c, c + MXU_DIM) for c in range(0, D_FF, MXU_DIM))
FUSE_CONV_FFN_MIN_SEQ = 4 * CONV_TILE
CONV_ROWS = 128
CONV_PAD = 32


def _dot(a, b):
    return jnp.dot(a, b, preferred_element_type=F32)


def _rms_norm(x, g):
    ms = jnp.mean(x * x, axis=-1, keepdims=True)
    return x * lax.rsqrt(ms + EPS) * g


def _const_spec(shape):
    zeros = (0,) * len(shape)
    return pl.BlockSpec(shape, lambda *_: zeros, pipeline_mode=pl.Buffered(1))


def _ffn_chunk(h, wg_ref, wu_ref, wd_ref, c0, c1):
    gate = _dot(h, wg_ref[:, c0:c1])
    up = _dot(h, wu_ref[:, c0:c1])
    a = (gate * jax.nn.sigmoid(gate) * up).astype(BF16)
    return _dot(a, wd_ref[c0:c1, :])


def _ffn_body(x_ref, g_ref, wg_ref, wu_ref, wd_ref, o_ref):
    x = x_ref[...]
    h = _rms_norm(x, g_ref[...]).astype(BF16)
    acc = None
    for c0, c1 in FF_CHUNKS:
        part = _ffn_chunk(h, wg_ref, wu_ref, wd_ref, c0, c1)
        acc = part if acc is None else acc + part
    o_ref[...] = x + 0.5 * acc


def _ffn(x, g, wg, wu, wd):
    b, t, d = x.shape
    rows = b * t
    tm = min(FFN_TILE, rows)
    assert rows % tm == 0
    out = pl.pallas_call(
        _ffn_body,
        grid=(rows // tm,),
        in_specs=[
            pl.BlockSpec((tm, d), lambda i: (i, 0)),
            _const_spec((1, d)),
            _const_spec((d, D_FF)),
            _const_spec((d, D_FF)),
            _const_spec((D_FF, d)),
        ],
        out_specs=pl.BlockSpec((tm, d), lambda i: (i, 0)),
        out_shape=jax.ShapeDtypeStruct((rows, d), F32),
        compiler_params=pltpu.CompilerParams(
            dimension_semantics=("arbitrary",), vmem_limit_bytes=VMEM_LIMIT_BYTES),
        name="ffn",
    )(x.reshape(rows, d), g.reshape(1, d), wg, wu, wd)
    return out.reshape(b, t, d)


def _head_mean_sq(y, bd_ref):
    sq = (y * y).astype(BF16)
    bd = bd_ref[...]
    cols = [_dot(sq[:, c:c + MXU_DIM], bd) for c in range(0, D_MODEL, MXU_DIM)]
    return jnp.concatenate(cols, axis=1)


def _attn_body(*refs, fresh, gps):
    if fresh:
        (x_ref, ng_ref, wqkv_ref, qg_ref, kg_ref, bias_ref, far_ref, wo_ref, bd_ref,
         xo_ref, ko_ref, vo_ref, hk_ref, hv_ref) = refs
        ck_ref = cv_ref = None
    else:
        (x_ref, ck_ref, cv_ref, ng_ref, wqkv_ref, qg_ref, kg_ref, bias_ref, far_ref, wo_ref,
         bd_ref, xo_ref, ko_ref, vo_ref, hk_ref, hv_ref) = refs
    t = pl.program_id(1)
    ring = N_SLOTS - 1 + gps

    lane = lax.broadcasted_iota(jnp.int32, (GROUP, D_MODEL), 1)
    even_head = (lane % LANES) < HEAD_DIM
    dim = lax.broadcasted_iota(jnp.int32, (D_MODEL, GROUP), 0)
    even_dim = (dim % LANES) < HEAD_DIM

    def put_v(slot, v):
        hv_ref[slot, 0:GROUP, :] = jnp.where(even_head, v, 0.0).astype(BF16)
        hv_ref[slot, GROUP:2 * GROUP, :] = jnp.where(even_head, 0.0, v).astype(BF16)

    def put_k(slot, k):
        kt = k.T
        hk_ref[slot, :, 0:GROUP] = jnp.where(even_dim, kt, 0.0).astype(BF16)
        hk_ref[slot, :, GROUP:2 * GROUP] = jnp.where(even_dim, 0.0, kt).astype(BF16)

    @pl.when(t == 0)
    def _init():
        if fresh:
            hk_ref[...] = jnp.zeros(hk_ref.shape, BF16)
            hv_ref[...] = jnp.zeros(hv_ref.shape, BF16)
        else:
            for d in range(1, N_SLOTS):
                r0 = LEFT_CTX - GROUP * d
                put_k(ring - d, ck_ref[0, r0:r0 + GROUP, :])
                put_v(ring - d, cv_ref[0, r0:r0 + GROUP, :])

    x = x_ref[0]
    h = _rms_norm(x, ng_ref[...]).astype(BF16)
    k = _dot(h, wqkv_ref[:, D_MODEL:2 * D_MODEL])
    kn = k * lax.rsqrt(_head_mean_sq(k, bd_ref) + EPS) * kg_ref[...]
    ko_ref[0] = kn
    for i in range(gps):
        put_k((t * gps + i) % ring, kn[GROUP * i:GROUP * (i + 1)])
    q = _dot(h, wqkv_ref[:, :D_MODEL])
    qn = q * lax.rsqrt(_head_mean_sq(q, bd_ref) + EPS) * qg_ref[...]
    qb = (qn * (ATTN_SCALE * LOG2_E)).astype(BF16)
    v = _dot(h, wqkv_ref[:, 2 * D_MODEL:])
    vo_ref[0] = v
    for i in range(gps):
        put_v((t * gps + i) % ring, v[GROUP * i:GROUP * (i + 1)])

    half_lane = lax.broadcasted_iota(jnp.int32, (CHUNK, 2 * GROUP), 1) % GROUP
    second_key_chunk = half_lane >= CHUNK
    key_row = lax.broadcasted_iota(jnp.int32, (2 * GROUP, LANES), 0)
    out_lane = lax.broadcasted_iota(jnp.int32, (2 * GROUP, LANES), 1)
    head_rows = ((key_row < GROUP) == (out_lane < HEAD_DIM)).astype(BF16)

    slots = [[(t * gps + i + ring - d) % ring for d in range(N_SLOTS)] for i in range(gps)]

    def scores(i, j):
        ls = slice(LANES * j, LANES * (j + 1))
        qp = qb[GROUP * i:GROUP * (i + 1), ls]
        s_even, s_odd = [], []
        for d in range(N_SLOTS):
            if d < 2:
                bias = bias_ref[d, j]
            elif fresh:
                bias = jnp.where(t * gps + i >= d, far_ref[j], NEG_INF)
            else:
                bias = far_ref[j]
            s = _dot(qp, hk_ref[slots[i][d], ls, :]) + bias
            if d == 0:
                s = jnp.concatenate(
                    [jnp.where(second_key_chunk, NEG_INF, s[:CHUNK]), s[CHUNK:]], axis=0)
            if d == N_SLOTS - 1:
                s = jnp.concatenate(
                    [s[:CHUNK], jnp.where(second_key_chunk, s[CHUNK:], NEG_INF)], axis=0)
            if fresh and d == 1:
                s = jnp.where(t * gps + i >= d, s, NEG_INF)
            s_even.append(s[:, :GROUP])
            s_odd.append(s[:, GROUP:])
        return s_even, s_odd

    def attend(i, j, s_even, s_odd):
        ls = slice(LANES * j, LANES * (j + 1))
        p_halves = []
        for tiles in (s_even, s_odd):
            m = functools.reduce(jnp.maximum, tiles)
            m = jnp.max(m, axis=-1, keepdims=True)
            p_halves.append([jnp.exp2(s - m) for s in tiles])
        o = None
        for d in range(N_SLOTS):
            p = jnp.concatenate([p_halves[0][d], p_halves[1][d]], axis=1).astype(BF16)
            part = _dot(p, jnp.concatenate([hv_ref[slots[i][d], :, ls], head_rows], axis=1))
            o = part if o is None else o + part
        return o[:, :LANES] / o[:, LANES:]

    work = [(i, j) for i in range(gps) for j in range(N_PAIRS)]
    o_cols = [[] for _ in range(gps)]
    pending = scores(*work[0])
    for n, (i, j) in enumerate(work):
        nxt = scores(*work[n + 1]) if n + 1 < len(work) else None
        o_cols[i].append(attend(i, j, *pending))
        pending = nxt
    o = jnp.concatenate([jnp.concatenate(c, axis=1) for c in o_cols], axis=0).astype(BF16)
    xo_ref[0] = x + _dot(o, wo_ref[...])


def _rel_bias_tiles(table):
    assert REL_CLIP == GROUP
    tt = table.T.astype(F32) * LOG2_E
    last = tt[:, 2 * REL_CLIP:]
    near = jnp.concatenate([tt[:, REL_CLIP:0:-1], tt[:, 2 * REL_CLIP:REL_CLIP:-1]], axis=1)
    prev = jnp.concatenate([tt[:, 2 * REL_CLIP:REL_CLIP:-1],
                            jnp.broadcast_to(last, (N_HEADS, GROUP))], axis=1)

    def toeplitz(w):
        flat = jnp.tile(w, (1, GROUP))[:, :GROUP * (2 * GROUP - 1)]
        return flat.reshape(N_HEADS, GROUP, 2 * GROUP - 1)[:, :, :GROUP]

    tiles = jnp.stack([toeplitz(near), toeplitz(prev)])
    tiles = tiles.reshape(2, N_PAIRS, 2, GROUP, GROUP).transpose(0, 1, 3, 2, 4)
    far = jnp.repeat(last.reshape(N_PAIRS, 1, 2), GROUP, axis=2)
    return tiles.reshape(2, N_PAIRS, GROUP, 2 * GROUP), far


def _attention(x, cache_k, cache_v, norm_g, w_qkv, q_gain, k_gain, rel_table, w_o):
    b, t, d = x.shape
    gps = ATTN_GROUPS_PER_STEP if t % (ATTN_GROUPS_PER_STEP * GROUP) == 0 else 1
    tq = gps * GROUP
    assert t % tq == 0 and (N_SLOTS - 1) % gps == 0
    n_t = t // tq
    keep = min((N_SLOTS - 1) // gps, n_t)
    fresh = cache_k is None
    bd = jnp.kron(jnp.eye(MXU_DIM // HEAD_DIM, dtype=F32),
                  jnp.full((HEAD_DIM, HEAD_DIM), 1.0 / HEAD_DIM, F32)).astype(BF16)
    x_spec = pl.BlockSpec((1, tq, d), lambda i, j: (i, j, 0))
    cache_spec = pl.BlockSpec((1, LEFT_CTX, d), lambda i, j: (i, 0, 0))
    kv_spec = pl.BlockSpec((1, tq, d), lambda i, j: (i, jnp.maximum(j - (n_t - keep), 0), 0))
    in_specs = [x_spec] + ([] if fresh else [cache_spec, cache_spec]) + [
        _const_spec((1, d)),
        _const_spec((d, 3 * d)),
        _const_spec((1, d)),
        _const_spec((1, d)),
        _const_spec((2, N_PAIRS, GROUP, 2 * GROUP)),
        _const_spec((N_PAIRS, 1, 2 * GROUP)),
        _const_spec((d, d)),
        _const_spec((MXU_DIM, MXU_DIM)),
    ]
    args = [x] + ([] if fresh else [cache_k, cache_v]) + [
        norm_g.reshape(1, d), w_qkv,
        jnp.tile(q_gain, N_HEADS).reshape(1, d), jnp.tile(k_gain, N_HEADS).reshape(1, d),
        *_rel_bias_tiles(rel_table), w_o, bd,
    ]
    kv_shape = jax.ShapeDtypeStruct((b, keep * tq, d), F32)
    return pl.pallas_call(
        functools.partial(_attn_body, fresh=fresh, gps=gps),
        grid=(b, n_t),
        in_specs=in_specs,
        out_specs=[x_spec, kv_spec, kv_spec],
        out_shape=[jax.ShapeDtypeStruct((b, t, d), F32), kv_shape, kv_shape],
        scratch_shapes=[pltpu.VMEM((N_SLOTS - 1 + gps, d, 2 * GROUP), BF16),
                        pltpu.VMEM((N_SLOTS - 1 + gps, 2 * GROUP, d), BF16)],
        compiler_params=pltpu.CompilerParams(
            dimension_semantics=("arbitrary", "arbitrary"), vmem_limit_bytes=VMEM_LIMIT_BYTES),
        name="attn_fresh" if fresh else "attn_cached",
    )(*args)


def _conv_load_history(t, st_ref, ext_ref, tm):
    first = CONV_PAD - CONV_HIST

    @pl.when(t == 0)
    def _load_state():
        ext_ref[0:first, :] = jnp.zeros((first, D_MODEL), F32)
        ext_ref[first:CONV_PAD, :] = st_ref[0]

    @pl.when(t > 0)
    def _carry_state():
        ext_ref[first:CONV_PAD, :] = ext_ref[tm + first:tm + CONV_PAD, :]


def _conv_front(x, ng_ref, w1_ref, b1_ref, ext_ref, tm):
    h = _rms_norm(x, ng_ref[...]).astype(BF16)
    ag = _dot(h, w1_ref[...]) + b1_ref[...]
    ext_ref[CONV_PAD:CONV_PAD + tm, :] = ag[:, :D_MODEL] * jax.nn.sigmoid(ag[:, D_MODEL:])


def _depthwise_blocks(ext_ref, wdw_ref, bdw_ref, y_ref, tm):
    first = CONV_PAD - CONV_HIST
    rows = min(CONV_ROWS, tm)
    jobs = []
    for r0 in range(0, tm, rows):
        for c0 in range(0, D_MODEL, LANES):
            def job(start=None, r0=r0, c0=c0):
                cols = slice(c0, c0 + LANES)
                out = jnp.broadcast_to(bdw_ref[:, cols], (rows, LANES))
                if start is not None:
                    out = out + start
                for res in range(SUBLANES):
                    n = rows + (SUBLANES if res else 0)
                    acc = None
                    for j in range(CONV_WIDTH):
                        if (first + j) % SUBLANES != res:
                            continue
                        base = r0 + first + j - res
                        term = ext_ref[base:base + n, cols] * wdw_ref[j:j + 1, cols]
                        acc = term if acc is None else acc + term
                    out = out + acc[res:res + rows]
                y_ref[r0:r0 + rows, cols] = out
                return out
            jobs.append(job)
    return jobs


def _conv_back(x, y_ref, lng_ref, lnb_ref, w2_ref, b2_ref):
    y = y_ref[...]
    mu = jnp.mean(y, axis=-1, keepdims=True)
    yc = y - mu
    var = jnp.mean(yc * yc, axis=-1, keepdims=True)
    yn = yc * lax.rsqrt(var + EPS) * lng_ref[...] + lnb_ref[...]
    z = (yn * jax.nn.sigmoid(yn)).astype(BF16)
    return x + _dot(z, w2_ref[...]) + b2_ref[...]


def _conv_body(x_ref, st_ref, ng_ref, w1_ref, b1_ref, wdw_ref, bdw_ref, lng_ref, lnb_ref,
               w2_ref, b2_ref, xo_ref, so_ref, ext_ref, y_ref, *, tm):
    t = pl.program_id(1)
    _conv_load_history(t, st_ref, ext_ref, tm)
    x = x_ref[0]
    _conv_front(x, ng_ref, w1_ref, b1_ref, ext_ref, tm)
    so_ref[0] = ext_ref[tm + CONV_PAD - CONV_HIST:tm + CONV_PAD, :]
    for job in _depthwise_blocks(ext_ref, wdw_ref, bdw_ref, y_ref, tm):
        job()
    xo_ref[0] = _conv_back(x, y_ref, lng_ref, lnb_ref, w2_ref, b2_ref)


def _zero_tied_to(dep, zero_ref):
    return pltpu.bitcast(pltpu.bitcast(dep, jnp.int32) & zero_ref[...], F32)


def _conv_ffn_body(x_ref, st_ref, zero_ref, ng_ref, w1_ref, b1_ref, wdw_ref, bdw_ref, lng_ref,
                   lnb_ref, w2_ref, b2_ref, fg_ref, wg_ref, wu_ref, wd_ref, o_ref, so_ref,
                   ext_ref, y_ref, x1_ref, *, tm, n_t):
    t = pl.program_id(1)
    _conv_load_history(t, st_ref, ext_ref, tm)

    @pl.when(t == 0)
    def _no_previous_tile():
        x1_ref[1] = jnp.zeros((tm, D_MODEL), F32)

    x = x_ref[0]
    xb = x1_ref[(t + 1) % 2]
    hb = _rms_norm(xb, fg_ref[...]).astype(BF16)
    _conv_front(x, ng_ref, w1_ref, b1_ref, ext_ref, tm)

    @pl.when(t == n_t - 1)
    def _new_state():
        so_ref[0] = ext_ref[tm + CONV_PAD - CONV_HIST:tm + CONV_PAD, :]

    jobs = _depthwise_blocks(ext_ref, wdw_ref, bdw_ref, y_ref, tm)
    per_round = -(-len(jobs) // len(FUSED_FF_CHUNKS))
    rows = min(CONV_ROWS, tm)
    acc = None
    ffn_zero = None
    dw_zero = None
    for c, (c0, c1) in enumerate(FUSED_FF_CHUNKS):
        h_c = hb
        if dw_zero is not None:
            z16 = jnp.tile(dw_zero, (2, 1)).astype(BF16)
            h_c = hb + jnp.tile(z16, (tm // z16.shape[0], D_MODEL // LANES))
        part = _ffn_chunk(h_c, wg_ref, wu_ref, wd_ref, c0, c1)
        acc = part if acc is None else acc + part
        start = None if ffn_zero is None else jnp.tile(ffn_zero, (rows // SUBLANES, 1))
        dep = None
        for job in jobs[c * per_round:(c + 1) * per_round]:
            out = job(start)
            blk = functools.reduce(jnp.add, [out[r:r + SUBLANES] for r in range(0, rows, SUBLANES)])
            dep = blk if dep is None else dep + blk
        if dep is not None:
            dw_zero = _zero_tied_to(dep, zero_ref)
        corners = (part[:SUBLANES, :LANES] + part[:SUBLANES, -LANES:]
                   + part[-SUBLANES:, :LANES] + part[-SUBLANES:, -LANES:])
        ffn_zero = _zero_tied_to(corners, zero_ref)
    o_ref[0] = xb + 0.5 * acc
    x1_ref[t % 2] = _conv_back(x, y_ref, lng_ref, lnb_ref, w2_ref, b2_ref)


def _conv_module(x, state, norm_g, w_pw1, b_pw1, w_dw, b_dw, ln_g, ln_b, w_pw2, b_pw2):
    b, t, d = x.shape
    tm = min(CONV_TILE, t)
    assert t % tm == 0 and tm % min(CONV_ROWS, tm) == 0 and tm >= CONV_HIST
    x_spec = pl.BlockSpec((1, tm, d), lambda i, j: (i, j, 0))
    st_spec = pl.BlockSpec((1, CONV_HIST, d), lambda i, j: (i, 0, 0))
    return pl.pallas_call(
        functools.partial(_conv_body, tm=tm),
        grid=(b, t // tm),
        in_specs=[
            x_spec, st_spec,
            _const_spec((1, d)),
            _const_spec((d, 2 * d)),
            _const_spec((1, 2 * d)),
            _const_spec((CONV_WIDTH, d)),
            _const_spec((1, d)),
            _const_spec((1, d)),
            _const_spec((1, d)),
            _const_spec((d, d)),
            _const_spec((1, d)),
        ],
        out_specs=[x_spec, st_spec],
        out_shape=[jax.ShapeDtypeStruct((b, t, d), F32),
                   jax.ShapeDtypeStruct((b, CONV_HIST, d), F32)],
        scratch_shapes=[pltpu.VMEM((CONV_PAD + tm, d), F32), pltpu.VMEM((tm, d), F32)],
        compiler_params=pltpu.CompilerParams(
            dimension_semantics=("arbitrary", "arbitrary"), vmem_limit_bytes=VMEM_LIMIT_BYTES),
        name="conv",
    )(x, state, norm_g.reshape(1, d), w_pw1, b_pw1.reshape(1, 2 * d), w_dw, b_dw.reshape(1, d),
      ln_g.reshape(1, d), ln_b.reshape(1, d), w_pw2, b_pw2.reshape(1, d))


def _conv_ffn(x, state, conv_p, ffn_p):
    norm_g, w_pw1, b_pw1, w_dw, b_dw, ln_g, ln_b, w_pw2, b_pw2 = conv_p
    g, wg, wu, wd = ffn_p
    b, t, d = x.shape
    tm = min(CONV_TILE, t)
    assert t % tm == 0 and tm % min(CONV_ROWS, tm) == 0 and tm >= CONV_HIST
    n_t = t // tm
    st_spec = pl.BlockSpec((1, CONV_HIST, d), lambda i, j: (i, 0, 0))
    return pl.pallas_call(
        functools.partial(_conv_ffn_body, tm=tm, n_t=n_t),
        grid=(b, n_t + 1),
        in_specs=[
            pl.BlockSpec((1, tm, d), lambda i, j: (i, jnp.minimum(j, n_t - 1), 0)),
            st_spec,
            _const_spec((SUBLANES, LANES)),
            _const_spec((1, d)),
            _const_spec((d, 2 * d)),
            _const_spec((1, 2 * d)),
            _const_spec((CONV_WIDTH, d)),
            _const_spec((1, d)),
            _const_spec((1, d)),
            _const_spec((1, d)),
            _const_spec((d, d)),
            _const_spec((1, d)),
            _const_spec((1, d)),
            _const_spec((d, D_FF)),
            _const_spec((d, D_FF)),
            _const_spec((D_FF, d)),
        ],
        out_specs=[pl.BlockSpec((1, tm, d), lambda i, j: (i, jnp.maximum(j - 1, 0), 0)), st_spec],
        out_shape=[jax.ShapeDtypeStruct((b, t, d), F32),
                   jax.ShapeDtypeStruct((b, CONV_HIST, d), F32)],
        scratch_shapes=[pltpu.VMEM((CONV_PAD + tm, d), F32), pltpu.VMEM((tm, d), F32),
                        pltpu.VMEM((2, tm, d), F32)],
        compiler_params=pltpu.CompilerParams(
            dimension_semantics=("arbitrary", "arbitrary"), vmem_limit_bytes=VMEM_LIMIT_BYTES),
        name="conv_ffn",
    )(x, state, jnp.zeros((SUBLANES, LANES), jnp.int32), norm_g.reshape(1, d), w_pw1,
      b_pw1.reshape(1, 2 * d), w_dw, b_dw.reshape(1, d), ln_g.reshape(1, d), ln_b.reshape(1, d),
      w_pw2, b_pw2.reshape(1, d), g.reshape(1, d), wg, wu, wd)


def _trunk(x, cache_k, cache_v, conv_state, p):
    seq = x.shape[1]
    x = _ffn(x, p['ffn1_norm'][0], p['ffn1_w_gate'][0], p['ffn1_w_up'][0], p['ffn1_w_down'][0])
    pad = (-seq) % GROUP
    xa = jnp.pad(x, ((0, 0), (0, pad), (0, 0))) if pad else x
    xa, k_new, v_new = _attention(
        xa, cache_k, cache_v, p['attn_norm'][0], p['attn_w_qkv'][0], p['attn_q_gain'][0],
        p['attn_k_gain'][0], p['attn_rel_bias'][0], p['attn_w_o'][0])
    x = xa[:, :seq] if pad else xa
    x = _ffn(x, p['ffn2_norm'][0], p['ffn2_w_gate'][0], p['ffn2_w_up'][0], p['ffn2_w_down'][0])
    x = _ffn(x, p['ffn1_norm'][1], p['ffn1_w_gate'][1], p['ffn1_w_up'][1], p['ffn1_w_down'][1])
    conv_p = (p['conv_norm'][0], p['conv_w_pw1'][0], p['conv_b_pw1'][0], p['conv_w_dw'][0],
              p['conv_b_dw'][0], p['conv_ln_g'][0], p['conv_ln_b'][0], p['conv_w_pw2'][0],
              p['conv_b_pw2'][0])
    ffn_p = (p['ffn2_norm'][1], p['ffn2_w_gate'][1], p['ffn2_w_up'][1], p['ffn2_w_down'][1])
    if seq >= FUSE_CONV_FFN_MIN_SEQ:
        x, conv_new = _conv_ffn(x, conv_state, conv_p, ffn_p)
    else:
        x, conv_new = _conv_module(x, conv_state, *conv_p)
        x = _ffn(x, *ffn_p)
    return x, k_new, v_new, conv_new


def kernel(x_prompt, x_sample, cache_attn_k, cache_attn_v, state_conv, ffn1_norm, ffn1_w_gate, ffn1_w_up, ffn1_w_down, ffn2_norm, ffn2_w_gate, ffn2_w_up, ffn2_w_down, attn_norm, attn_w_qkv, attn_q_gain, attn_k_gain, attn_rel_bias, attn_w_o, conv_norm, conv_w_pw1, conv_b_pw1, conv_w_dw, conv_b_dw, conv_ln_g, conv_ln_b, conv_w_pw2, conv_b_pw2):
    p = dict(
        ffn1_norm=ffn1_norm, ffn2_norm=ffn2_norm, attn_norm=attn_norm, attn_q_gain=attn_q_gain,
        attn_k_gain=attn_k_gain, attn_rel_bias=attn_rel_bias, conv_norm=conv_norm,
        conv_b_pw1=conv_b_pw1, conv_w_dw=conv_w_dw, conv_b_dw=conv_b_dw, conv_ln_g=conv_ln_g,
        conv_ln_b=conv_ln_b, conv_b_pw2=conv_b_pw2,
        ffn1_w_gate=ffn1_w_gate.astype(BF16), ffn1_w_up=ffn1_w_up.astype(BF16),
        ffn1_w_down=ffn1_w_down.astype(BF16), ffn2_w_gate=ffn2_w_gate.astype(BF16),
        ffn2_w_up=ffn2_w_up.astype(BF16), ffn2_w_down=ffn2_w_down.astype(BF16),
        attn_w_qkv=attn_w_qkv.astype(BF16), attn_w_o=attn_w_o.astype(BF16),
        conv_w_pw1=conv_w_pw1.astype(BF16), conv_w_pw2=conv_w_pw2.astype(BF16))

    bp, sp, d = x_prompt.shape
    bs, ss, _ = x_sample.shape
    assert attn_w_qkv.shape[0] == 1 and conv_w_pw1.shape[0] == 1

    zero_conv = jnp.zeros((bp, CONV_HIST, d), F32)
    y_p, k_p, v_p, conv_p = _trunk(x_prompt, None, None, zero_conv, p)
    keep = min(LEFT_CTX, sp)
    k_p = k_p[:, k_p.shape[1] - keep:].reshape(1, bp, keep, N_HEADS, HEAD_DIM)
    v_p = v_p[:, v_p.shape[1] - keep:].reshape(1, bp, keep, N_HEADS, HEAD_DIM)

    y_s, k_s, v_s, conv_s = _trunk(
        x_sample, cache_attn_k[0].reshape(bs, LEFT_CTX, d), cache_attn_v[0].reshape(bs, LEFT_CTX, d),
        state_conv[0], p)
    k_s = k_s[:, :ss].reshape(1, bs, ss, N_HEADS, HEAD_DIM)
    v_s = v_s[:, :ss].reshape(1, bs, ss, N_HEADS, HEAD_DIM)
    return (y_p, y_s, k_p, v_p, k_s, v_s, conv_p[None], conv_s[None])
```

```python
import functools

import jax
import jax.numpy as jnp
from jax import lax
from jax.experimental import pallas as pl
from jax.experimental.pallas import tpu as pltpu

D_MODEL = 1024
N_HEADS = 16
HEAD_DIM = D_MODEL // N_HEADS
CHUNK = 64
LEFT_CHUNKS = 8
LEFT_CTX = LEFT_CHUNKS * CHUNK
REL_CLIP = 128
CONV_WIDTH = 31
CONV_HIST = CONV_WIDTH - 1
D_FF = 2816
EPS = 1e-6
NEG_INF = -1e30

BF16 = jnp.bfloat16
F32 = jnp.float32

LANES = 128
SUBLANES = 8
MXU_DIM = 256
VMEM_LIMIT_BYTES = 56 * 1024 * 1024

GROUP = 2 * CHUNK
N_SLOTS = LEFT_CTX // GROUP + 1
N_PAIRS = N_HEADS // 2
ATTN_SCALE = HEAD_DIM ** -0.5
LOG2_E = 1.4426950408889634
ATTN_GROUPS_PER_STEP = 4

FFN_TILE = 512
FF_CHUNKS = tuple((c, c + MXU_DIM) for c in range(0, D_FF, MXU_DIM))
CONV_TILE = 512
FUSE_CONV_FFN_MIN_SEQ = 4 * CONV_TILE
CONV_ROWS = 128
CONV_PAD = 32


def _dot(a, b):
    return jnp.dot(a, b, preferred_element_type=F32)


def _rms_norm(x, g):
    ms = jnp.mean(x * x, axis=-1, keepdims=True)
    return x * lax.rsqrt(ms + EPS) * g


def _const_spec(shape):
    zeros = (0,) * len(shape)
    return pl.BlockSpec(shape, lambda *_: zeros, pipeline_mode=pl.Buffered(1))


def _ffn_chunk(h, wg_ref, wu_ref, wd_ref, c0, c1):
    gate = _dot(h, wg_ref[:, c0:c1])
    up = _dot(h, wu_ref[:, c0:c1])
    a = (gate * jax.nn.sigmoid(gate) * up).astype(BF16)
    return _dot(a, wd_ref[c0:c1, :])


def _ffn_tile(x, g_ref, wg_ref, wu_ref, wd_ref):
    h = _rms_norm(x, g_ref[...]).astype(BF16)
    acc = None
    for c0, c1 in FF_CHUNKS:
        part = _ffn_chunk(h, wg_ref, wu_ref, wd_ref, c0, c1)
        acc = part if acc is None else acc + part
    return x + 0.5 * acc


def _ffn_body(x_ref, g_ref, wg_ref, wu_ref, wd_ref, o_ref):
    o_ref[...] = _ffn_tile(x_ref[...], g_ref, wg_ref, wu_ref, wd_ref)


def _ffn_tail_body(x_ref, xt_ref, g_ref, wg_ref, wu_ref, wd_ref, o_ref, ot_ref, *, n_main):
    is_main = pl.program_id(0) < n_main

    @pl.when(is_main)
    def _main():
        o_ref[...] = _ffn_tile(x_ref[...], g_ref, wg_ref, wu_ref, wd_ref)

    @pl.when(jnp.logical_not(is_main))
    def _tail():
        ot_ref[...] = _ffn_tile(xt_ref[...], g_ref, wg_ref, wu_ref, wd_ref)


def _ffn(x, g, wg, wu, wd):
    b, t, d = x.shape
    rows = b * t
    tm = min(FFN_TILE, rows)
    assert rows % tm == 0
    out = pl.pallas_call(
        _ffn_body,
        grid=(rows // tm,),
        in_specs=[
            pl.BlockSpec((tm, d), lambda i: (i, 0)),
            _const_spec((1, d)),
            _const_spec((d, D_FF)),
            _const_spec((d, D_FF)),
            _const_spec((D_FF, d)),
        ],
        out_specs=pl.BlockSpec((tm, d), lambda i: (i, 0)),
        out_shape=jax.ShapeDtypeStruct((rows, d), F32),
        compiler_params=pltpu.CompilerParams(
            dimension_semantics=("arbitrary",), vmem_limit_bytes=VMEM_LIMIT_BYTES),
        name="ffn",
    )(x.reshape(rows, d), g.reshape(1, d), wg, wu, wd)
    return out.reshape(b, t, d)


def _ffn_with_tail(x, xt, g, wg, wu, wd):
    b, t, d = x.shape
    rows = b * t
    tm = FFN_TILE
    n_main = rows // tm
    assert rows % tm == 0 and xt.shape[0] * xt.shape[1] == tm
    main_spec = pl.BlockSpec((tm, d), lambda i: (jnp.minimum(i, n_main - 1), 0))
    out, out_t = pl.pallas_call(
        functools.partial(_ffn_tail_body, n_main=n_main),
        grid=(n_main + 1,),
        in_specs=[
            main_spec,
            _const_spec((tm, d)),
            _const_spec((1, d)),
            _const_spec((d, D_FF)),
            _const_spec((d, D_FF)),
            _const_spec((D_FF, d)),
        ],
        out_specs=[main_spec, pl.BlockSpec((tm, d), lambda i: (0, 0))],
        out_shape=[jax.ShapeDtypeStruct((rows, d), F32), jax.ShapeDtypeStruct((tm, d), F32)],
        compiler_params=pltpu.CompilerParams(
            dimension_semantics=("arbitrary",), vmem_limit_bytes=VMEM_LIMIT_BYTES),
        name="ffn_tail",
    )(x.reshape(rows, d), xt.reshape(tm, d), g.reshape(1, d), wg, wu, wd)
    return out.reshape(b, t, d), out_t.reshape(xt.shape)


def _ffn_both(x, xt, ffn_p):
    if xt.shape[0] * xt.shape[1] == FFN_TILE and (x.shape[0] * x.shape[1]) % FFN_TILE == 0:
        return _ffn_with_tail(x, xt, *ffn_p)
    return _ffn(x, *ffn_p), _ffn(xt, *ffn_p)


def _head_mean_sq(y, bd_ref):
    sq = (y * y).astype(BF16)
    bd = bd_ref[...]
    cols = [_dot(sq[:, c:c + MXU_DIM], bd) for c in range(0, D_MODEL, MXU_DIM)]
    return jnp.concatenate(cols, axis=1)


def _attn_body(*refs, fresh, gps, valid):
    if fresh:
        (x_ref, ng_ref, wqkv_ref, qg_ref, kg_ref, bias_ref, far_ref, wo_ref, bd_ref,
         xo_ref, ko_ref, vo_ref, hk_ref, hv_ref) = refs
        ck_ref = cv_ref = None
    else:
        (x_ref, ck_ref, cv_ref, ng_ref, wqkv_ref, qg_ref, kg_ref, bias_ref, far_ref, wo_ref,
         bd_ref, xo_ref, ko_ref, vo_ref, hk_ref, hv_ref) = refs
    t = pl.program_id(1)
    ring = N_SLOTS - 1 + gps

    lane = lax.broadcasted_iota(jnp.int32, (GROUP, D_MODEL), 1)
    even_head = (lane % LANES) < HEAD_DIM
    dim = lax.broadcasted_iota(jnp.int32, (D_MODEL, GROUP), 0)
    even_dim = (dim % LANES) < HEAD_DIM

    def put_v(slot, v):
        hv_ref[slot, 0:GROUP, :] = jnp.where(even_head, v, 0.0).astype(BF16)
        hv_ref[slot, GROUP:2 * GROUP, :] = jnp.where(even_head, 0.0, v).astype(BF16)

    def put_k(slot, k):
        kt = k.T
        hk_ref[slot, :, 0:GROUP] = jnp.where(even_dim, kt, 0.0).astype(BF16)
        hk_ref[slot, :, GROUP:2 * GROUP] = jnp.where(even_dim, 0.0, kt).astype(BF16)

    @pl.when(t == 0)
    def _init():
        if fresh:
            hk_ref[...] = jnp.zeros(hk_ref.shape, BF16)
            hv_ref[...] = jnp.zeros(hv_ref.shape, BF16)
        else:
            for d in range(1, N_SLOTS):
                r0 = LEFT_CTX - GROUP * d
                put_k(ring - d, ck_ref[0, r0:r0 + GROUP, :])
                put_v(ring - d, cv_ref[0, r0:r0 + GROUP, :])

    x = x_ref[0]
    if valid < gps * GROUP:
        x = jnp.concatenate([x, jnp.zeros((gps * GROUP - valid, D_MODEL), F32)], axis=0)
    h = _rms_norm(x, ng_ref[...]).astype(BF16)
    k = _dot(h, wqkv_ref[:, D_MODEL:2 * D_MODEL])
    kn = k * lax.rsqrt(_head_mean_sq(k, bd_ref) + EPS) * kg_ref[...]
    ko_ref[0] = kn[:valid]
    for i in range(gps):
        put_k((t * gps + i) % ring, kn[GROUP * i:GROUP * (i + 1)])
    q = _dot(h, wqkv_ref[:, :D_MODEL])
    qn = q * lax.rsqrt(_head_mean_sq(q, bd_ref) + EPS) * qg_ref[...]
    qb = (qn * (ATTN_SCALE * LOG2_E)).astype(BF16)
    v = _dot(h, wqkv_ref[:, 2 * D_MODEL:])
    vo_ref[0] = v[:valid]
    for i in range(gps):
        put_v((t * gps + i) % ring, v[GROUP * i:GROUP * (i + 1)])

    half_lane = lax.broadcasted_iota(jnp.int32, (CHUNK, 2 * GROUP), 1) % GROUP
    second_key_chunk = half_lane >= CHUNK
    key_row = lax.broadcasted_iota(jnp.int32, (2 * GROUP, LANES), 0)
    out_lane = lax.broadcasted_iota(jnp.int32, (2 * GROUP, LANES), 1)
    head_rows = ((key_row < GROUP) == (out_lane < HEAD_DIM)).astype(BF16)

    slots = [[(t * gps + i + ring - d) % ring for d in range(N_SLOTS)] for i in range(gps)]

    def scores(i, j):
        ls = slice(LANES * j, LANES * (j + 1))
        qp = qb[GROUP * i:GROUP * (i + 1), ls]
        s_even, s_odd = [], []
        for d in range(N_SLOTS):
            if d < 2:
                bias = bias_ref[d, j]
            elif fresh:
                bias = jnp.where(t * gps + i >= d, far_ref[j], NEG_INF)
            else:
                bias = far_ref[j]
            s = _dot(qp, hk_ref[slots[i][d], ls, :]) + bias
            if d == 0:
                s = jnp.concatenate(
                    [jnp.where(second_key_chunk, NEG_INF, s[:CHUNK]), s[CHUNK:]], axis=0)
            if d == N_SLOTS - 1:
                s = jnp.concatenate(
                    [s[:CHUNK], jnp.where(second_key_chunk, s[CHUNK:], NEG_INF)], axis=0)
            if fresh and d == 1:
                s = jnp.where(t * gps + i >= d, s, NEG_INF)
            s_even.append(s[:, :GROUP])
            s_odd.append(s[:, GROUP:])
        return s_even, s_odd

    def attend(i, j, s_even, s_odd):
        ls = slice(LANES * j, LANES * (j + 1))
        p_halves = []
        for tiles in (s_even, s_odd):
            m = functools.reduce(jnp.maximum, tiles)
            m = jnp.max(m, axis=-1, keepdims=True)
            p_halves.append([jnp.exp2(s - m) for s in tiles])
        o = None
        for d in range(N_SLOTS):
            p = jnp.concatenate([p_halves[0][d], p_halves[1][d]], axis=1).astype(BF16)
            part = _dot(p, jnp.concatenate([hv_ref[slots[i][d], :, ls], head_rows], axis=1))
            o = part if o is None else o + part
        return o[:, :LANES] / o[:, LANES:]

    work = [(i, j) for i in range(gps) for j in range(N_PAIRS)]
    o_cols = [[] for _ in range(gps)]
    pending = scores(*work[0])
    for n, (i, j) in enumerate(work):
        nxt = scores(*work[n + 1]) if n + 1 < len(work) else None
        o_cols[i].append(attend(i, j, *pending))
        pending = nxt
    o = jnp.concatenate([jnp.concatenate(c, axis=1) for c in o_cols], axis=0).astype(BF16)
    xo_ref[0] = (x + _dot(o, wo_ref[...]))[:valid]


def _rel_bias_tiles(table):
    assert REL_CLIP == GROUP
    tt = table.T.astype(F32) * LOG2_E
    last = tt[:, 2 * REL_CLIP:]
    near = jnp.concatenate([tt[:, REL_CLIP:0:-1], tt[:, 2 * REL_CLIP:REL_CLIP:-1]], axis=1)
    prev = jnp.concatenate([tt[:, 2 * REL_CLIP:REL_CLIP:-1],
                            jnp.broadcast_to(last, (N_HEADS, GROUP))], axis=1)

    def toeplitz(w):
        flat = jnp.tile(w, (1, GROUP))[:, :GROUP * (2 * GROUP - 1)]
        return flat.reshape(N_HEADS, GROUP, 2 * GROUP - 1)[:, :, :GROUP]

    tiles = jnp.stack([toeplitz(near), toeplitz(prev)])
    tiles = tiles.reshape(2, N_PAIRS, 2, GROUP, GROUP).transpose(0, 1, 3, 2, 4)
    far = jnp.repeat(last.reshape(N_PAIRS, 1, 2), GROUP, axis=2)
    return tiles.reshape(2, N_PAIRS, GROUP, 2 * GROUP), far


def _attention(x, cache_k, cache_v, norm_g, w_qkv, q_gain, k_gain, rel_table, w_o):
    b, t, d = x.shape
    gps = ATTN_GROUPS_PER_STEP if t % (ATTN_GROUPS_PER_STEP * GROUP) == 0 else 1
    valid = min(t, gps * GROUP)
    assert t % valid == 0 and (t >= GROUP or valid % SUBLANES == 0) and (N_SLOTS - 1) % gps == 0
    n_t = t // valid
    keep = min((N_SLOTS - 1) // gps, n_t)
    fresh = cache_k is None
    bd = jnp.kron(jnp.eye(MXU_DIM // HEAD_DIM, dtype=F32),
                  jnp.full((HEAD_DIM, HEAD_DIM), 1.0 / HEAD_DIM, F32)).astype(BF16)
    x_spec = pl.BlockSpec((1, valid, d), lambda i, j: (i, j, 0))
    cache_spec = pl.BlockSpec((1, LEFT_CTX, d), lambda i, j: (i, 0, 0))
    kv_spec = pl.BlockSpec((1, valid, d), lambda i, j: (i, jnp.maximum(j - (n_t - keep), 0), 0))
    in_specs = [x_spec] + ([] if fresh else [cache_spec, cache_spec]) + [
        _const_spec((1, d)),
        _const_spec((d, 3 * d)),
        _const_spec((1, d)),
        _const_spec((1, d)),
        _const_spec((2, N_PAIRS, GROUP, 2 * GROUP)),
        _const_spec((N_PAIRS, 1, 2 * GROUP)),
        _const_spec((d, d)),
        _const_spec((MXU_DIM, MXU_DIM)),
    ]
    args = [x] + ([] if fresh else [cache_k, cache_v]) + [
        norm_g.reshape(1, d), w_qkv,
        jnp.tile(q_gain, N_HEADS).reshape(1, d), jnp.tile(k_gain, N_HEADS).reshape(1, d),
        *_rel_bias_tiles(rel_table), w_o, bd,
    ]
    kv_shape = jax.ShapeDtypeStruct((b, keep * valid, d), F32)
    return pl.pallas_call(
        functools.partial(_attn_body, fresh=fresh, gps=gps, valid=valid),
        grid=(b, n_t),
        in_specs=in_specs,
        out_specs=[x_spec, kv_spec, kv_spec],
        out_shape=[jax.ShapeDtypeStruct((b, t, d), F32), kv_shape, kv_shape],
        scratch_shapes=[pltpu.VMEM((N_SLOTS - 1 + gps, d, 2 * GROUP), BF16),
                        pltpu.VMEM((N_SLOTS - 1 + gps, 2 * GROUP, d), BF16)],
        compiler_params=pltpu.CompilerParams(
            dimension_semantics=("arbitrary", "arbitrary"), vmem_limit_bytes=VMEM_LIMIT_BYTES),
        name="attn_fresh" if fresh else "attn_cached",
    )(*args)


def _conv_load_history(t, st_ref, ext_ref, tm):
    first = CONV_PAD - CONV_HIST

    @pl.when(t == 0)
    def _load_state():
        ext_ref[0:first, :] = jnp.zeros((first, D_MODEL), F32)
        ext_ref[first:CONV_PAD, :] = st_ref[0]

    @pl.when(t > 0)
    def _carry_state():
        ext_ref[first:CONV_PAD, :] = ext_ref[tm + first:tm + CONV_PAD, :]


def _conv_front(x, ng_ref, w1_ref, b1_ref, ext_ref, tm):
    h = _rms_norm(x, ng_ref[...]).astype(BF16)
    ag = _dot(h, w1_ref[...]) + b1_ref[...]
    ext_ref[CONV_PAD:CONV_PAD + tm, :] = ag[:, :D_MODEL] * jax.nn.sigmoid(ag[:, D_MODEL:])


def _depthwise_blocks(ext_ref, wdw_ref, bdw_ref, y_ref, tm):
    first = CONV_PAD - CONV_HIST
    rows = min(CONV_ROWS, tm)
    jobs = []
    for r0 in range(0, tm, rows):
        for c0 in range(0, D_MODEL, LANES):
            def job(start=None, r0=r0, c0=c0):
                cols = slice(c0, c0 + LANES)
                out = jnp.broadcast_to(bdw_ref[:, cols], (rows, LANES))
                if start is not None:
                    out = out + start
                for res in range(SUBLANES):
                    n = rows + (SUBLANES if res else 0)
                    acc = None
                    for j in range(CONV_WIDTH):
                        if (first + j) % SUBLANES != res:
                            continue
                        base = r0 + first + j - res
                        term = ext_ref[base:base + n, cols] * wdw_ref[j:j + 1, cols]
                        acc = term if acc is None else acc + term
                    out = out + acc[res:res + rows]
                y_ref[r0:r0 + rows, cols] = out
                return out
            jobs.append(job)
    return jobs


def _conv_back(x, y_ref, lng_ref, lnb_ref, w2_ref, b2_ref):
    y = y_ref[...]
    mu = jnp.mean(y, axis=-1, keepdims=True)
    yc = y - mu
    var = jnp.mean(yc * yc, axis=-1, keepdims=True)
    yn = yc * lax.rsqrt(var + EPS) * lng_ref[...] + lnb_ref[...]
    z = (yn * jax.nn.sigmoid(yn)).astype(BF16)
    return x + _dot(z, w2_ref[...]) + b2_ref[...]


def _conv_body(x_ref, st_ref, ng_ref, w1_ref, b1_ref, wdw_ref, bdw_ref, lng_ref, lnb_ref,
               w2_ref, b2_ref, xo_ref, so_ref, ext_ref, y_ref, *, tm):
    t = pl.program_id(1)
    _conv_load_history(t, st_ref, ext_ref, tm)
    x = x_ref[0]
    _conv_front(x, ng_ref, w1_ref, b1_ref, ext_ref, tm)
    so_ref[0] = ext_ref[tm + CONV_PAD - CONV_HIST:tm + CONV_PAD, :]
    for job in _depthwise_blocks(ext_ref, wdw_ref, bdw_ref, y_ref, tm):
        job()
    xo_ref[0] = _conv_back(x, y_ref, lng_ref, lnb_ref, w2_ref, b2_ref)


def _zero_tied_to(dep, zero_ref):
    return pltpu.bitcast(pltpu.bitcast(dep, jnp.int32) & zero_ref[...], F32)


def _conv_ffn_body(x_ref, st_ref, zero_ref, ng_ref, w1_ref, b1_ref, wdw_ref, bdw_ref, lng_ref,
                   lnb_ref, w2_ref, b2_ref, fg_ref, wg_ref, wu_ref, wd_ref, o_ref, so_ref,
                   ext_ref, y_ref, x1_ref, *, tm, n_t):
    t = pl.program_id(1)
    _conv_load_history(t, st_ref, ext_ref, tm)

    @pl.when(t == 0)
    def _no_previous_tile():
        x1_ref[1] = jnp.zeros((tm, D_MODEL), F32)

    x = x_ref[0]
    xb = x1_ref[(t + 1) % 2]
    hb = _rms_norm(xb, fg_ref[...]).astype(BF16)
    _conv_front(x, ng_ref, w1_ref, b1_ref, ext_ref, tm)

    @pl.when(t == n_t - 1)
    def _new_state():
        so_ref[0] = ext_ref[tm + CONV_PAD - CONV_HIST:tm + CONV_PAD, :]

    jobs = _depthwise_blocks(ext_ref, wdw_ref, bdw_ref, y_ref, tm)
    per_round = -(-len(jobs) // len(FF_CHUNKS))
    rows = min(CONV_ROWS, tm)
    acc = None
    ffn_zero = None
    dw_zero = None
    for c, (c0, c1) in enumerate(FF_CHUNKS):
        h_c = hb
        if dw_zero is not None:
            z16 = jnp.tile(dw_zero, (2, 1)).astype(BF16)
            h_c = hb + jnp.tile(z16, (tm // z16.shape[0], D_MODEL // LANES))
        part = _ffn_chunk(h_c, wg_ref, wu_ref, wd_ref, c0, c1)
        acc = part if acc is None else acc + part
        start = None if ffn_zero is None else jnp.tile(ffn_zero, (rows // SUBLANES, 1))
        dep = None
        for job in jobs[c * per_round:(c + 1) * per_round]:
            out = job(start)
            blk = functools.reduce(jnp.add, [out[r:r + SUBLANES] for r in range(0, rows, SUBLANES)])
            dep = blk if dep is None else dep + blk
        if dep is not None:
            dw_zero = _zero_tied_to(dep, zero_ref)
        corners = (part[:SUBLANES, :LANES] + part[:SUBLANES, -LANES:]
                   + part[-SUBLANES:, :LANES] + part[-SUBLANES:, -LANES:])
        ffn_zero = _zero_tied_to(corners, zero_ref)
    o_ref[0] = xb + 0.5 * acc
    x1_ref[t % 2] = _conv_back(x, y_ref, lng_ref, lnb_ref, w2_ref, b2_ref)


def _conv_module(x, state, norm_g, w_pw1, b_pw1, w_dw, b_dw, ln_g, ln_b, w_pw2, b_pw2):
    b, t, d = x.shape
    tm = min(CONV_TILE, t)
    assert t % tm == 0 and tm % min(CONV_ROWS, tm) == 0 and tm >= CONV_HIST
    x_spec = pl.BlockSpec((1, tm, d), lambda i, j: (i, j, 0))
    st_spec = pl.BlockSpec((1, CONV_HIST, d), lambda i, j: (i, 0, 0))
    return pl.pallas_call(
        functools.partial(_conv_body, tm=tm),
        grid=(b, t // tm),
        in_specs=[
            x_spec, st_spec,
            _const_spec((1, d)),
            _const_spec((d, 2 * d)),
            _const_spec((1, 2 * d)),
            _const_spec((CONV_WIDTH, d)),
            _const_spec((1, d)),
            _const_spec((1, d)),
            _const_spec((1, d)),
            _const_spec((d, d)),
            _const_spec((1, d)),
        ],
        out_specs=[x_spec, st_spec],
        out_shape=[jax.ShapeDtypeStruct((b, t, d), F32),
                   jax.ShapeDtypeStruct((b, CONV_HIST, d), F32)],
        scratch_shapes=[pltpu.VMEM((CONV_PAD + tm, d), F32), pltpu.VMEM((tm, d), F32)],
        compiler_params=pltpu.CompilerParams(
            dimension_semantics=("arbitrary", "arbitrary"), vmem_limit_bytes=VMEM_LIMIT_BYTES),
        name="conv",
    )(x, state, norm_g.reshape(1, d), w_pw1, b_pw1.reshape(1, 2 * d), w_dw, b_dw.reshape(1, d),
      ln_g.reshape(1, d), ln_b.reshape(1, d), w_pw2, b_pw2.reshape(1, d))


def _conv_ffn(x, state, conv_p, ffn_p):
    norm_g, w_pw1, b_pw1, w_dw, b_dw, ln_g, ln_b, w_pw2, b_pw2 = conv_p
    g, wg, wu, wd = ffn_p
    b, t, d = x.shape
    tm = min(CONV_TILE, t)
    assert t % tm == 0 and tm % min(CONV_ROWS, tm) == 0 and tm >= CONV_HIST
    n_t = t // tm
    st_spec = pl.BlockSpec((1, CONV_HIST, d), lambda i, j: (i, 0, 0))
    return pl.pallas_call(
        functools.partial(_conv_ffn_body, tm=tm, n_t=n_t),
        grid=(b, n_t + 1),
        in_specs=[
            pl.BlockSpec((1, tm, d), lambda i, j: (i, jnp.minimum(j, n_t - 1), 0)),
            st_spec,
            _const_spec((SUBLANES, LANES)),
            _const_spec((1, d)),
            _const_spec((d, 2 * d)),
            _const_spec((1, 2 * d)),
            _const_spec((CONV_WIDTH, d)),
            _const_spec((1, d)),
            _const_spec((1, d)),
            _const_spec((1, d)),
            _const_spec((d, d)),
            _const_spec((1, d)),
            _const_spec((1, d)),
            _const_spec((d, D_FF)),
            _const_spec((d, D_FF)),
            _const_spec((D_FF, d)),
        ],
        out_specs=[pl.BlockSpec((1, tm, d), lambda i, j: (i, jnp.maximum(j - 1, 0), 0)), st_spec],
        out_shape=[jax.ShapeDtypeStruct((b, t, d), F32),
                   jax.ShapeDtypeStruct((b, CONV_HIST, d), F32)],
        scratch_shapes=[pltpu.VMEM((CONV_PAD + tm, d), F32), pltpu.VMEM((tm, d), F32),
                        pltpu.VMEM((2, tm, d), F32)],
        compiler_params=pltpu.CompilerParams(
            dimension_semantics=("arbitrary", "arbitrary"), vmem_limit_bytes=VMEM_LIMIT_BYTES),
        name="conv_ffn",
    )(x, state, jnp.zeros((SUBLANES, LANES), jnp.int32), norm_g.reshape(1, d), w_pw1,
      b_pw1.reshape(1, 2 * d), w_dw, b_dw.reshape(1, d), ln_g.reshape(1, d), ln_b.reshape(1, d),
      w_pw2, b_pw2.reshape(1, d), g.reshape(1, d), wg, wu, wd)


def _forward(x_prompt, x_sample, cache_k, cache_v, conv_prompt, conv_sample, p):
    def ffn_p(name, layer):
        return tuple(p[f'{name}_{w}'][layer] for w in ('norm', 'w_gate', 'w_up', 'w_down'))

    attn_p = (p['attn_norm'][0], p['attn_w_qkv'][0], p['attn_q_gain'][0], p['attn_k_gain'][0],
              p['attn_rel_bias'][0], p['attn_w_o'][0])
    conv_p = (p['conv_norm'][0], p['conv_w_pw1'][0], p['conv_b_pw1'][0], p['conv_w_dw'][0],
              p['conv_b_dw'][0], p['conv_ln_g'][0], p['conv_ln_b'][0], p['conv_w_pw2'][0],
              p['conv_b_pw2'][0])

    def conv_then_ffn(x, state):
        if x.shape[1] >= FUSE_CONV_FFN_MIN_SEQ:
            return _conv_ffn(x, state, conv_p, ffn_p('ffn2', 1))
        x, state = _conv_module(x, state, *conv_p)
        return _ffn(x, *ffn_p('ffn2', 1)), state

    xp, xs = _ffn_both(x_prompt, x_sample, ffn_p('ffn1', 0))
    xp, k_p, v_p = _attention(xp, None, None, *attn_p)
    xs, k_s, v_s = _attention(xs, cache_k, cache_v, *attn_p)
    xp, xs = _ffn_both(xp, xs, ffn_p('ffn2', 0))
    xp, xs = _ffn_both(xp, xs, ffn_p('ffn1', 1))
    xp, conv_p_new = conv_then_ffn(xp, conv_prompt)
    xs, conv_s_new = conv_then_ffn(xs, conv_sample)
    return xp, k_p, v_p, conv_p_new, xs, k_s, v_s, conv_s_new


def kernel(x_prompt, x_sample, cache_attn_k, cache_attn_v, state_conv, ffn1_norm, ffn1_w_gate, ffn1_w_up, ffn1_w_down, ffn2_norm, ffn2_w_gate, ffn2_w_up, ffn2_w_down, attn_norm, attn_w_qkv, attn_q_gain, attn_k_gain, attn_rel_bias, attn_w_o, conv_norm, conv_w_pw1, conv_b_pw1, conv_w_dw, conv_b_dw, conv_ln_g, conv_ln_b, conv_w_pw2, conv_b_pw2):
    p = dict(
        ffn1_norm=ffn1_norm, ffn2_norm=ffn2_norm, attn_norm=attn_norm, attn_q_gain=attn_q_gain,
        attn_k_gain=attn_k_gain, attn_rel_bias=attn_rel_bias, conv_norm=conv_norm,
        conv_b_pw1=conv_b_pw1, conv_w_dw=conv_w_dw, conv_b_dw=conv_b_dw, conv_ln_g=conv_ln_g,
        conv_ln_b=conv_ln_b, conv_b_pw2=conv_b_pw2,
        ffn1_w_gate=ffn1_w_gate.astype(BF16), ffn1_w_up=ffn1_w_up.astype(BF16),
        ffn1_w_down=ffn1_w_down.astype(BF16), ffn2_w_gate=ffn2_w_gate.astype(BF16),
        ffn2_w_up=ffn2_w_up.astype(BF16), ffn2_w_down=ffn2_w_down.astype(BF16),
        attn_w_qkv=attn_w_qkv.astype(BF16), attn_w_o=attn_w_o.astype(BF16),
        conv_w_pw1=conv_w_pw1.astype(BF16), conv_w_pw2=conv_w_pw2.astype(BF16))

    bp, sp, d = x_prompt.shape
    bs, ss, _ = x_sample.shape
    assert attn_w_qkv.shape[0] == 1 and conv_w_pw1.shape[0] == 1

    zero_conv = jnp.zeros((bp, CONV_HIST, d), F32)
    y_p, k_p, v_p, conv_p, y_s, k_s, v_s, conv_s = _forward(
        x_prompt, x_sample, cache_attn_k[0].reshape(bs, LEFT_CTX, d),
        cache_attn_v[0].reshape(bs, LEFT_CTX, d), zero_conv, state_conv[0], p)
    keep = min(LEFT_CTX, sp)
    k_p = k_p[:, k_p.shape[1] - keep:].reshape(1, bp, keep, N_HEADS, HEAD_DIM)
    v_p = v_p[:, v_p.shape[1] - keep:].reshape(1, bp, keep, N_HEADS, HEAD_DIM)
    k_s = k_s[:, :ss].reshape(1, bs, ss, N_HEADS, HEAD_DIM)
    v_s = v_s[:, :ss].reshape(1, bs, ss, N_HEADS, HEAD_DIM)
    return (y_p, y_s, k_p, v_p, k_s, v_s, conv_p[None], conv_s[None])
```

```python
import functools

import jax
import jax.numpy as jnp
from jax import lax
from jax.experimental import pallas as pl
from jax.experimental.pallas import tpu as pltpu

D_MODEL = 1024
N_HEADS = 16
HEAD_DIM = D_MODEL // N_HEADS
CHUNK = 64
LEFT_CHUNKS = 8
LEFT_CTX = LEFT_CHUNKS * CHUNK
REL_CLIP = 128
CONV_WIDTH = 31
CONV_HIST = CONV_WIDTH - 1
D_FF = 2816
EPS = 1e-6
NEG_INF = -1e30

BF16 = jnp.bfloat16
F32 = jnp.float32

LANES = 128
SUBLANES = 8
MXU_DIM = 256
VMEM_LIMIT_BYTES = 56 * 1024 * 1024

GROUP = 2 * CHUNK
N_SLOTS = LEFT_CTX // GROUP + 1
N_PAIRS = N_HEADS // 2
ATTN_SCALE = HEAD_DIM ** -0.5
LOG2_E = 1.4426950408889634
ATTN_GROUPS_PER_STEP = 4

FFN_TILE = 512
FF_CHUNKS = tuple((c, c + MXU_DIM) for c in range(0, D_FF, MXU_DIM))
CONV_TILE = 512
FUSE_CONV_FFN_MIN_SEQ = 4 * CONV_TILE
CONV_ROWS = 128
CONV_PAD = 32


def _dot(a, b):
    return jnp.dot(a, b, preferred_element_type=F32)


def _rms_norm(x, g):
    ms = jnp.mean(x * x, axis=-1, keepdims=True)
    return x * lax.rsqrt(ms + EPS) * g


def _const_spec(shape):
    zeros = (0,) * len(shape)
    return pl.BlockSpec(shape, lambda *_: zeros, pipeline_mode=pl.Buffered(1))


def _ffn_chunk(h, wg_ref, wu_ref, wd_ref, c0, c1):
    gate = _dot(h, wg_ref[:, c0:c1])
    up = _dot(h, wu_ref[:, c0:c1])
    a = (gate * jax.nn.sigmoid(gate) * up).astype(BF16)
    return _dot(a, wd_ref[c0:c1, :])


def _ffn_tile(x, g_ref, wg_ref, wu_ref, wd_ref):
    h = _rms_norm(x, g_ref[...]).astype(BF16)
    acc = None
    for c0, c1 in FF_CHUNKS:
        part = _ffn_chunk(h, wg_ref, wu_ref, wd_ref, c0, c1)
        acc = part if acc is None else acc + part
    return x + 0.5 * acc


def _ffn_body(x_ref, g_ref, wg_ref, wu_ref, wd_ref, o_ref):
    o_ref[...] = _ffn_tile(x_ref[...], g_ref, wg_ref, wu_ref, wd_ref)


def _ffn_tail_body(x_ref, xt_ref, g_ref, wg_ref, wu_ref, wd_ref, o_ref, ot_ref, *, n_main):
    is_main = pl.program_id(0) < n_main

    @pl.when(is_main)
    def _main():
        o_ref[...] = _ffn_tile(x_ref[...], g_ref, wg_ref, wu_ref, wd_ref)

    @pl.when(jnp.logical_not(is_main))
    def _tail():
        ot_ref[...] = _ffn_tile(xt_ref[...], g_ref, wg_ref, wu_ref, wd_ref)


def _ffn(x, g, wg, wu, wd):
    b, t, d = x.shape
    rows = b * t
    tm = min(FFN_TILE, rows)
    assert rows % tm == 0
    out = pl.pallas_call(
        _ffn_body,
        grid=(rows // tm,),
        in_specs=[
            pl.BlockSpec((tm, d), lambda i: (i, 0)),
            _const_spec((1, d)),
            _const_spec((d, D_FF)),
            _const_spec((d, D_FF)),
            _const_spec((D_FF, d)),
        ],
        out_specs=pl.BlockSpec((tm, d), lambda i: (i, 0)),
        out_shape=jax.ShapeDtypeStruct((rows, d), F32),
        compiler_params=pltpu.CompilerParams(
            dimension_semantics=("arbitrary",), vmem_limit_bytes=VMEM_LIMIT_BYTES),
        name="ffn",
    )(x.reshape(rows, d), g.reshape(1, d), wg, wu, wd)
    return out.reshape(b, t, d)


def _ffn_with_tail(x, xt, g, wg, wu, wd):
    b, t, d = x.shape
    rows = b * t
    tm = FFN_TILE
    n_main = rows // tm
    assert rows % tm == 0 and xt.shape[0] * xt.shape[1] == tm
    main_spec = pl.BlockSpec((tm, d), lambda i: (jnp.minimum(i, n_main - 1), 0))
    out, out_t = pl.pallas_call(
        functools.partial(_ffn_tail_body, n_main=n_main),
        grid=(n_main + 1,),
        in_specs=[
            main_spec,
            _const_spec((tm, d)),
            _const_spec((1, d)),
            _const_spec((d, D_FF)),
            _const_spec((d, D_FF)),
            _const_spec((D_FF, d)),
        ],
        out_specs=[main_spec, pl.BlockSpec((tm, d), lambda i: (0, 0))],
        out_shape=[jax.ShapeDtypeStruct((rows, d), F32), jax.ShapeDtypeStruct((tm, d), F32)],
        compiler_params=pltpu.CompilerParams(
            dimension_semantics=("arbitrary",), vmem_limit_bytes=VMEM_LIMIT_BYTES),
        name="ffn_tail",
    )(x.reshape(rows, d), xt.reshape(tm, d), g.reshape(1, d), wg, wu, wd)
    return out.reshape(b, t, d), out_t.reshape(xt.shape)


def _ffn_both(x, xt, ffn_p):
    if xt.shape[0] * xt.shape[1] == FFN_TILE and (x.shape[0] * x.shape[1]) % FFN_TILE == 0:
        return _ffn_with_tail(x, xt, *ffn_p)
    return _ffn(x, *ffn_p), _ffn(xt, *ffn_p)


def _head_mean_sq(y, bd_ref):
    sq = (y * y).astype(BF16)
    bd = bd_ref[...]
    cols = [_dot(sq[:, c:c + MXU_DIM], bd) for c in range(0, D_MODEL, MXU_DIM)]
    return jnp.concatenate(cols, axis=1)


def _attn_body(*refs, fresh, gps, valid, n_cast):
    n_in = 9 if fresh else 11
    cast_in = refs[n_in:n_in + n_cast]
    xo_ref, ko_ref, vo_ref = refs[n_in + n_cast:n_in + n_cast + 3]
    cast_out = refs[n_in + n_cast + 3:n_in + 2 * n_cast + 3]
    hk_ref, hv_ref = refs[n_in + 2 * n_cast + 3:]
    if fresh:
        x_ref, ng_ref, wqkv_ref, qg_ref, kg_ref, bias_ref, far_ref, wo_ref, bd_ref = refs[:n_in]
        ck_ref = cv_ref = None
    else:
        (x_ref, ck_ref, cv_ref, ng_ref, wqkv_ref, qg_ref, kg_ref, bias_ref, far_ref, wo_ref,
         bd_ref) = refs[:n_in]
    t = pl.program_id(1)
    for w_ref, wb_ref in zip(cast_in, cast_out):
        wb_ref[...] = w_ref[0].astype(BF16)
    ring = N_SLOTS - 1 + gps

    lane = lax.broadcasted_iota(jnp.int32, (GROUP, D_MODEL), 1)
    even_head = (lane % LANES) < HEAD_DIM
    dim = lax.broadcasted_iota(jnp.int32, (D_MODEL, GROUP), 0)
    even_dim = (dim % LANES) < HEAD_DIM

    def put_v(slot, v):
        hv_ref[slot, 0:GROUP, :] = jnp.where(even_head, v, 0.0).astype(BF16)
        hv_ref[slot, GROUP:2 * GROUP, :] = jnp.where(even_head, 0.0, v).astype(BF16)

    def put_k(slot, k):
        kt = k.T
        hk_ref[slot, :, 0:GROUP] = jnp.where(even_dim, kt, 0.0).astype(BF16)
        hk_ref[slot, :, GROUP:2 * GROUP] = jnp.where(even_dim, 0.0, kt).astype(BF16)

    @pl.when(t == 0)
    def _init():
        if fresh:
            hk_ref[...] = jnp.zeros(hk_ref.shape, BF16)
            hv_ref[...] = jnp.zeros(hv_ref.shape, BF16)
        else:
            for d in range(1, N_SLOTS):
                r0 = LEFT_CTX - GROUP * d
                put_k(ring - d, ck_ref[0, r0:r0 + GROUP, :])
                put_v(ring - d, cv_ref[0, r0:r0 + GROUP, :])

    x = x_ref[0]
    if valid < gps * GROUP:
        x = jnp.concatenate([x, jnp.zeros((gps * GROUP - valid, D_MODEL), F32)], axis=0)
    h = _rms_norm(x, ng_ref[...]).astype(BF16)
    k = _dot(h, wqkv_ref[:, D_MODEL:2 * D_MODEL])
    kn = k * lax.rsqrt(_head_mean_sq(k, bd_ref) + EPS) * kg_ref[...]
    ko_ref[0] = kn[:valid]
    for i in range(gps):
        put_k((t * gps + i) % ring, kn[GROUP * i:GROUP * (i + 1)])
    q = _dot(h, wqkv_ref[:, :D_MODEL])
    qn = q * lax.rsqrt(_head_mean_sq(q, bd_ref) + EPS) * qg_ref[...]
    qb = (qn * (ATTN_SCALE * LOG2_E)).astype(BF16)
    v = _dot(h, wqkv_ref[:, 2 * D_MODEL:])
    vo_ref[0] = v[:valid]
    for i in range(gps):
        put_v((t * gps + i) % ring, v[GROUP * i:GROUP * (i + 1)])

    half_lane = lax.broadcasted_iota(jnp.int32, (CHUNK, 2 * GROUP), 1) % GROUP
    second_key_chunk = half_lane >= CHUNK
    key_row = lax.broadcasted_iota(jnp.int32, (2 * GROUP, LANES), 0)
    out_lane = lax.broadcasted_iota(jnp.int32, (2 * GROUP, LANES), 1)
    head_rows = ((key_row < GROUP) == (out_lane < HEAD_DIM)).astype(BF16)

    slots = [[(t * gps + i + ring - d) % ring for d in range(N_SLOTS)] for i in range(gps)]

    def scores(i, j):
        ls = slice(LANES * j, LANES * (j + 1))
        qp = qb[GROUP * i:GROUP * (i + 1), ls]
        s_even, s_odd = [], []
        for d in range(N_SLOTS):
            if d < 2:
                bias = bias_ref[d, j]
            elif fresh:
                bias = jnp.where(t * gps + i >= d, far_ref[j], NEG_INF)
            else:
                bias = far_ref[j]
            s = _dot(qp, hk_ref[slots[i][d], ls, :]) + bias
            if d == 0:
                s = jnp.concatenate(
                    [jnp.where(second_key_chunk, NEG_INF, s[:CHUNK]), s[CHUNK:]], axis=0)
            if d == N_SLOTS - 1:
                s = jnp.concatenate(
                    [s[:CHUNK], jnp.where(second_key_chunk, s[CHUNK:], NEG_INF)], axis=0)
            if fresh and d == 1:
                s = jnp.where(t * gps + i >= d, s, NEG_INF)
            s_even.append(s[:, :GROUP])
            s_odd.append(s[:, GROUP:])
        return s_even, s_odd

    def attend(i, j, s_even, s_odd):
        ls = slice(LANES * j, LANES * (j + 1))
        p_halves = []
        for tiles in (s_even, s_odd):
            m = functools.reduce(jnp.maximum, tiles)
            m = jnp.max(m, axis=-1, keepdims=True)
            p_halves.append([jnp.exp2(s - m) for s in tiles])
        o = None
        for d in range(N_SLOTS):
            p = jnp.concatenate([p_halves[0][d], p_halves[1][d]], axis=1).astype(BF16)
            part = _dot(p, jnp.concatenate([hv_ref[slots[i][d], :, ls], head_rows], axis=1))
            o = part if o is None else o + part
        return o[:, :LANES] / o[:, LANES:]

    work = [(i, j) for i in range(gps) for j in range(N_PAIRS)]
    o_cols = [[] for _ in range(gps)]
    pending = scores(*work[0])
    for n, (i, j) in enumerate(work):
        nxt = scores(*work[n + 1]) if n + 1 < len(work) else None
        o_cols[i].append(attend(i, j, *pending))
        pending = nxt
    o = jnp.concatenate([jnp.concatenate(c, axis=1) for c in o_cols], axis=0).astype(BF16)
    xo_ref[0] = (x + _dot(o, wo_ref[...]))[:valid]


def _rel_bias_tiles(table):
    assert REL_CLIP == GROUP
    tt = table.T.astype(F32) * LOG2_E
    last = tt[:, 2 * REL_CLIP:]
    near = jnp.concatenate([tt[:, REL_CLIP:0:-1], tt[:, 2 * REL_CLIP:REL_CLIP:-1]], axis=1)
    prev = jnp.concatenate([tt[:, 2 * REL_CLIP:REL_CLIP:-1],
                            jnp.broadcast_to(last, (N_HEADS, GROUP))], axis=1)

    def toeplitz(w):
        flat = jnp.tile(w, (1, GROUP))[:, :GROUP * (2 * GROUP - 1)]
        return flat.reshape(N_HEADS, GROUP, 2 * GROUP - 1)[:, :, :GROUP]

    tiles = jnp.stack([toeplitz(near), toeplitz(prev)])
    tiles = tiles.reshape(2, N_PAIRS, 2, GROUP, GROUP).transpose(0, 1, 3, 2, 4)
    far = jnp.repeat(last.reshape(N_PAIRS, 1, 2), GROUP, axis=2)
    return tiles.reshape(2, N_PAIRS, GROUP, 2 * GROUP), far


def _cast_blocks(w, layer, n_t, n_steps):
    _, rows, cols = w.shape
    r = next(m for m in range(2 * SUBLANES, rows + 1, 2 * SUBLANES)
             if rows % m == 0 and rows // m <= n_steps)
    last = rows // r - 1
    in_spec = pl.BlockSpec((1, r, cols), lambda i, j: (layer, jnp.minimum(i * n_t + j, last), 0))
    out_spec = pl.BlockSpec((r, cols), lambda i, j: (jnp.minimum(i * n_t + j, last), 0))
    return in_spec, out_spec, jax.ShapeDtypeStruct((rows, cols), BF16)


def _attention(x, cache_k, cache_v, norm_g, w_qkv, q_gain, k_gain, rel_table, w_o, casts=()):
    b, t, d = x.shape
    gps = ATTN_GROUPS_PER_STEP if t % (ATTN_GROUPS_PER_STEP * GROUP) == 0 else 1
    valid = min(t, gps * GROUP)
    assert t % valid == 0 and (t >= GROUP or valid % SUBLANES == 0) and (N_SLOTS - 1) % gps == 0
    n_t = t // valid
    keep = min((N_SLOTS - 1) // gps, n_t)
    fresh = cache_k is None
    bd = jnp.kron(jnp.eye(MXU_DIM // HEAD_DIM, dtype=F32),
                  jnp.full((HEAD_DIM, HEAD_DIM), 1.0 / HEAD_DIM, F32)).astype(BF16)
    x_spec = pl.BlockSpec((1, valid, d), lambda i, j: (i, j, 0))
    cache_spec = pl.BlockSpec((1, LEFT_CTX, d), lambda i, j: (i, 0, 0))
    kv_spec = pl.BlockSpec((1, valid, d), lambda i, j: (i, jnp.maximum(j - (n_t - keep), 0), 0))
    in_specs = [x_spec] + ([] if fresh else [cache_spec, cache_spec]) + [
        _const_spec((1, d)),
        _const_spec((d, 3 * d)),
        _const_spec((1, d)),
        _const_spec((1, d)),
        _const_spec((2, N_PAIRS, GROUP, 2 * GROUP)),
        _const_spec((N_PAIRS, 1, 2 * GROUP)),
        _const_spec((d, d)),
        _const_spec((MXU_DIM, MXU_DIM)),
    ]
    args = [x] + ([] if fresh else [cache_k, cache_v]) + [
        norm_g.reshape(1, d), w_qkv,
        jnp.tile(q_gain, N_HEADS).reshape(1, d), jnp.tile(k_gain, N_HEADS).reshape(1, d),
        *_rel_bias_tiles(rel_table), w_o, bd,
    ]
    kv_shape = jax.ShapeDtypeStruct((b, keep * valid, d), F32)
    cast_specs = [_cast_blocks(w, layer, n_t, b * n_t) for w, layer in casts]
    return pl.pallas_call(
        functools.partial(_attn_body, fresh=fresh, gps=gps, valid=valid, n_cast=len(casts)),
        grid=(b, n_t),
        in_specs=in_specs + [c[0] for c in cast_specs],
        out_specs=[x_spec, kv_spec, kv_spec] + [c[1] for c in cast_specs],
        out_shape=[jax.ShapeDtypeStruct((b, t, d), F32), kv_shape, kv_shape]
        + [c[2] for c in cast_specs],
        scratch_shapes=[pltpu.VMEM((N_SLOTS - 1 + gps, d, 2 * GROUP), BF16),
                        pltpu.VMEM((N_SLOTS - 1 + gps, 2 * GROUP, d), BF16)],
        compiler_params=pltpu.CompilerParams(
            dimension_semantics=("arbitrary", "arbitrary"), vmem_limit_bytes=VMEM_LIMIT_BYTES),
        name="attn_fresh" if fresh else "attn_cached",
    )(*args, *[w for w, _ in casts])


def _conv_load_history(t, st_ref, ext_ref, tm):
    first = CONV_PAD - CONV_HIST

    @pl.when(t == 0)
    def _load_state():
        ext_ref[0:first, :] = jnp.zeros((first, D_MODEL), F32)
        ext_ref[first:CONV_PAD, :] = st_ref[0]

    @pl.when(t > 0)
    def _carry_state():
        ext_ref[first:CONV_PAD, :] = ext_ref[tm + first:tm + CONV_PAD, :]


def _conv_front(x, ng_ref, w1_ref, b1_ref, ext_ref, tm):
    h = _rms_norm(x, ng_ref[...]).astype(BF16)
    ag = _dot(h, w1_ref[...]) + b1_ref[...]
    ext_ref[CONV_PAD:CONV_PAD + tm, :] = ag[:, :D_MODEL] * jax.nn.sigmoid(ag[:, D_MODEL:])


def _depthwise_blocks(ext_ref, wdw_ref, bdw_ref, y_ref, tm):
    first = CONV_PAD - CONV_HIST
    rows = min(CONV_ROWS, tm)
    jobs = []
    for r0 in range(0, tm, rows):
        for c0 in range(0, D_MODEL, LANES):
            def job(start=None, r0=r0, c0=c0):
                cols = slice(c0, c0 + LANES)
                out = jnp.broadcast_to(bdw_ref[:, cols], (rows, LANES))
                if start is not None:
                    out = out + start
                for res in range(SUBLANES):
                    n = rows + (SUBLANES if res else 0)
                    acc = None
                    for j in range(CONV_WIDTH):
                        if (first + j) % SUBLANES != res:
                            continue
                        base = r0 + first + j - res
                        term = ext_ref[base:base + n, cols] * wdw_ref[j:j + 1, cols]
                        acc = term if acc is None else acc + term
                    out = out + acc[res:res + rows]
                y_ref[r0:r0 + rows, cols] = out
                return out
            jobs.append(job)
    return jobs


def _conv_back(x, y_ref, lng_ref, lnb_ref, w2_ref, b2_ref):
    y = y_ref[...]
    mu = jnp.mean(y, axis=-1, keepdims=True)
    yc = y - mu
    var = jnp.mean(yc * yc, axis=-1, keepdims=True)
    yn = yc * lax.rsqrt(var + EPS) * lng_ref[...] + lnb_ref[...]
    z = (yn * jax.nn.sigmoid(yn)).astype(BF16)
    return x + _dot(z, w2_ref[...]) + b2_ref[...]


def _conv_body(x_ref, st_ref, ng_ref, w1_ref, b1_ref, wdw_ref, bdw_ref, lng_ref, lnb_ref,
               w2_ref, b2_ref, xo_ref, so_ref, ext_ref, y_ref, *, tm):
    t = pl.program_id(1)
    _conv_load_history(t, st_ref, ext_ref, tm)
    x = x_ref[0]
    _conv_front(x, ng_ref, w1_ref, b1_ref, ext_ref, tm)
    so_ref[0] = ext_ref[tm + CONV_PAD - CONV_HIST:tm + CONV_PAD, :]
    for job in _depthwise_blocks(ext_ref, wdw_ref, bdw_ref, y_ref, tm):
        job()
    xo_ref[0] = _conv_back(x, y_ref, lng_ref, lnb_ref, w2_ref, b2_ref)


def _zero_tied_to(dep, zero_ref):
    return pltpu.bitcast(pltpu.bitcast(dep, jnp.int32) & zero_ref[...], F32)


def _conv_ffn_body(x_ref, st_ref, zero_ref, ng_ref, w1_ref, b1_ref, wdw_ref, bdw_ref, lng_ref,
                   lnb_ref, w2_ref, b2_ref, fg_ref, wg_ref, wu_ref, wd_ref, o_ref, so_ref,
                   ext_ref, y_ref, x1_ref, *, tm, n_t):
    t = pl.program_id(1)
    _conv_load_history(t, st_ref, ext_ref, tm)

    @pl.when(t == 0)
    def _no_previous_tile():
        x1_ref[1] = jnp.zeros((tm, D_MODEL), F32)

    x = x_ref[0]
    xb = x1_ref[(t + 1) % 2]
    hb = _rms_norm(xb, fg_ref[...]).astype(BF16)
    _conv_front(x, ng_ref, w1_ref, b1_ref, ext_ref, tm)

    @pl.when(t == n_t - 1)
    def _new_state():
        so_ref[0] = ext_ref[tm + CONV_PAD - CONV_HIST:tm + CONV_PAD, :]

    jobs = _depthwise_blocks(ext_ref, wdw_ref, bdw_ref, y_ref, tm)
    per_round = -(-len(jobs) // len(FF_CHUNKS))
    rows = min(CONV_ROWS, tm)
    acc = None
    ffn_zero = None
    dw_zero = None
    for c, (c0, c1) in enumerate(FF_CHUNKS):
        h_c = hb
        if dw_zero is not None:
            z16 = jnp.tile(dw_zero, (2, 1)).astype(BF16)
            h_c = hb + jnp.tile(z16, (tm // z16.shape[0], D_MODEL // LANES))
        part = _ffn_chunk(h_c, wg_ref, wu_ref, wd_ref, c0, c1)
        acc = part if acc is None else acc + part
        start = None if ffn_zero is None else jnp.tile(ffn_zero, (rows // SUBLANES, 1))
        dep = None
        for job in jobs[c * per_round:(c + 1) * per_round]:
            out = job(start)
            blk = functools.reduce(jnp.add, [out[r:r + SUBLANES] for r in range(0, rows, SUBLANES)])
            dep = blk if dep is None else dep + blk
        if dep is not None:
            dw_zero = _zero_tied_to(dep, zero_ref)
        corners = (part[:SUBLANES, :LANES] + part[:SUBLANES, -LANES:]
                   + part[-SUBLANES:, :LANES] + part[-SUBLANES:, -LANES:])
        ffn_zero = _zero_tied_to(corners, zero_ref)
    o_ref[0] = xb + 0.5 * acc
    x1_ref[t % 2] = _conv_back(x, y_ref, lng_ref, lnb_ref, w2_ref, b2_ref)


def _conv_module(x, state, norm_g, w_pw1, b_pw1, w_dw, b_dw, ln_g, ln_b, w_pw2, b_pw2):
    b, t, d = x.shape
    tm = min(CONV_TILE, t)
    assert t % tm == 0 and tm % min(CONV_ROWS, tm) == 0 and tm >= CONV_HIST
    x_spec = pl.BlockSpec((1, tm, d), lambda i, j: (i, j, 0))
    st_spec = pl.BlockSpec((1, CONV_HIST, d), lambda i, j: (i, 0, 0))
    return pl.pallas_call(
        functools.partial(_conv_body, tm=tm),
        grid=(b, t // tm),
        in_specs=[
            x_spec, st_spec,
            _const_spec((1, d)),
            _const_spec((d, 2 * d)),
            _const_spec((1, 2 * d)),
            _const_spec((CONV_WIDTH, d)),
            _const_spec((1, d)),
            _const_spec((1, d)),
            _const_spec((1, d)),
            _const_spec((d, d)),
            _const_spec((1, d)),
        ],
        out_specs=[x_spec, st_spec],
        out_shape=[jax.ShapeDtypeStruct((b, t, d), F32),
                   jax.ShapeDtypeStruct((b, CONV_HIST, d), F32)],
        scratch_shapes=[pltpu.VMEM((CONV_PAD + tm, d), F32), pltpu.VMEM((tm, d), F32)],
        compiler_params=pltpu.CompilerParams(
            dimension_semantics=("arbitrary", "arbitrary"), vmem_limit_bytes=VMEM_LIMIT_BYTES),
        name="conv",
    )(x, state, norm_g.reshape(1, d), w_pw1, b_pw1.reshape(1, 2 * d), w_dw, b_dw.reshape(1, d),
      ln_g.reshape(1, d), ln_b.reshape(1, d), w_pw2, b_pw2.reshape(1, d))


def _conv_ffn(x, state, conv_p, ffn_p):
    norm_g, w_pw1, b_pw1, w_dw, b_dw, ln_g, ln_b, w_pw2, b_pw2 = conv_p
    g, wg, wu, wd = ffn_p
    b, t, d = x.shape
    tm = min(CONV_TILE, t)
    assert t % tm == 0 and tm % min(CONV_ROWS, tm) == 0 and tm >= CONV_HIST
    n_t = t // tm
    st_spec = pl.BlockSpec((1, CONV_HIST, d), lambda i, j: (i, 0, 0))
    return pl.pallas_call(
        functools.partial(_conv_ffn_body, tm=tm, n_t=n_t),
        grid=(b, n_t + 1),
        in_specs=[
            pl.BlockSpec((1, tm, d), lambda i, j: (i, jnp.minimum(j, n_t - 1), 0)),
            st_spec,
            _const_spec((SUBLANES, LANES)),
            _const_spec((1, d)),
            _const_spec((d, 2 * d)),
            _const_spec((1, 2 * d)),
            _const_spec((CONV_WIDTH, d)),
            _const_spec((1, d)),
            _const_spec((1, d)),
            _const_spec((1, d)),
            _const_spec((d, d)),
            _const_spec((1, d)),
            _const_spec((1, d)),
            _const_spec((d, D_FF)),
            _const_spec((d, D_FF)),
            _const_spec((D_FF, d)),
        ],
        out_specs=[pl.BlockSpec((1, tm, d), lambda i, j: (i, jnp.maximum(j - 1, 0), 0)), st_spec],
        out_shape=[jax.ShapeDtypeStruct((b, t, d), F32),
                   jax.ShapeDtypeStruct((b, CONV_HIST, d), F32)],
        scratch_shapes=[pltpu.VMEM((CONV_PAD + tm, d), F32), pltpu.VMEM((tm, d), F32),
                        pltpu.VMEM((2, tm, d), F32)],
        compiler_params=pltpu.CompilerParams(
            dimension_semantics=("arbitrary", "arbitrary"), vmem_limit_bytes=VMEM_LIMIT_BYTES),
        name="conv_ffn",
    )(x, state, jnp.zeros((SUBLANES, LANES), jnp.int32), norm_g.reshape(1, d), w_pw1,
      b_pw1.reshape(1, 2 * d), w_dw, b_dw.reshape(1, d), ln_g.reshape(1, d), ln_b.reshape(1, d),
      w_pw2, b_pw2.reshape(1, d), g.reshape(1, d), wg, wu, wd)


_DEFERRED = (('ffn2', 0), ('ffn1', 1), ('ffn2', 1))


def _forward(x_prompt, x_sample, cache_k, cache_v, conv_prompt, conv_sample, p):
    ffn_w = ('w_gate', 'w_up', 'w_down')
    ffn1_0 = (p['ffn1_norm'][0],) + tuple(p[f'ffn1_{w}'][0].astype(BF16) for w in ffn_w)
    attn_p = (p['attn_norm'][0], p['attn_w_qkv'][0].astype(BF16), p['attn_q_gain'][0],
              p['attn_k_gain'][0], p['attn_rel_bias'][0], p['attn_w_o'][0].astype(BF16))
    casts = [(p[f'{name}_{w}'], layer) for name, layer in _DEFERRED for w in ffn_w]
    casts += [(p['conv_w_pw1'], 0), (p['conv_w_pw2'], 0)]

    xp, xs = _ffn_both(x_prompt, x_sample, ffn1_0)
    xp, k_p, v_p, *bf = _attention(xp, None, None, *attn_p, casts=casts)
    xs, k_s, v_s = _attention(xs, cache_k, cache_v, *attn_p)
    ffn_p = {key: (p[f'{key[0]}_norm'][key[1]],) + tuple(bf[3 * n:3 * n + 3])
             for n, key in enumerate(_DEFERRED)}
    w_pw1, w_pw2 = bf[3 * len(_DEFERRED):]
    conv_p = (p['conv_norm'][0], w_pw1, p['conv_b_pw1'][0], p['conv_w_dw'][0], p['conv_b_dw'][0],
              p['conv_ln_g'][0], p['conv_ln_b'][0], w_pw2, p['conv_b_pw2'][0])

    def conv_then_ffn(x, state):
        if x.shape[1] >= FUSE_CONV_FFN_MIN_SEQ:
            return _conv_ffn(x, state, conv_p, ffn_p['ffn2', 1])
        x, state = _conv_module(x, state, *conv_p)
        return _ffn(x, *ffn_p['ffn2', 1]), state

    xp, xs = _ffn_both(xp, xs, ffn_p['ffn2', 0])
    xp, xs = _ffn_both(xp, xs, ffn_p['ffn1', 1])
    xp, conv_p_new = conv_then_ffn(xp, conv_prompt)
    xs, conv_s_new = conv_then_ffn(xs, conv_sample)
    return xp, k_p, v_p, conv_p_new, xs, k_s, v_s, conv_s_new


def kernel(x_prompt, x_sample, cache_attn_k, cache_attn_v, state_conv, ffn1_norm, ffn1_w_gate, ffn1_w_up, ffn1_w_down, ffn2_norm, ffn2_w_gate, ffn2_w_up, ffn2_w_down, attn_norm, attn_w_qkv, attn_q_gain, attn_k_gain, attn_rel_bias, attn_w_o, conv_norm, conv_w_pw1, conv_b_pw1, conv_w_dw, conv_b_dw, conv_ln_g, conv_ln_b, conv_w_pw2, conv_b_pw2):
    p = dict(
        ffn1_norm=ffn1_norm, ffn1_w_gate=ffn1_w_gate, ffn1_w_up=ffn1_w_up, ffn1_w_down=ffn1_w_down,
        ffn2_norm=ffn2_norm, ffn2_w_gate=ffn2_w_gate, ffn2_w_up=ffn2_w_up, ffn2_w_down=ffn2_w_down,
        attn_norm=attn_norm, attn_w_qkv=attn_w_qkv, attn_q_gain=attn_q_gain, attn_k_gain=attn_k_gain,
        attn_rel_bias=attn_rel_bias, attn_w_o=attn_w_o, conv_norm=conv_norm, conv_w_pw1=conv_w_pw1,
        conv_b_pw1=conv_b_pw1, conv_w_dw=conv_w_dw, conv_b_dw=conv_b_dw, conv_ln_g=conv_ln_g,
        conv_ln_b=conv_ln_b, conv_w_pw2=conv_w_pw2, conv_b_pw2=conv_b_pw2)

    bp, sp, d = x_prompt.shape
    bs, ss, _ = x_sample.shape
    assert attn_w_qkv.shape[0] == 1 and conv_w_pw1.shape[0] == 1

    zero_conv = jnp.zeros((bp, CONV_HIST, d), F32)
    y_p, k_p, v_p, conv_p, y_s, k_s, v_s, conv_s = _forward(
        x_prompt, x_sample, cache_attn_k[0].reshape(bs, LEFT_CTX, d),
        cache_attn_v[0].reshape(bs, LEFT_CTX, d), zero_conv, state_conv[0], p)
    keep = min(LEFT_CTX, sp)
    k_p = k_p[:, k_p.shape[1] - keep:].reshape(1, bp, keep, N_HEADS, HEAD_DIM)
    v_p = v_p[:, v_p.shape[1] - keep:].reshape(1, bp, keep, N_HEADS, HEAD_DIM)
    k_s = k_s[:, :ss].reshape(1, bs, ss, N_HEADS, HEAD_DIM)
    v_s = v_s[:, :ss].reshape(1, bs, ss, N_HEADS, HEAD_DIM)
    return (y_p, y_s, k_p, v_p, k_s, v_s, conv_p[None], conv_s[None])
```
